```python
import math
import jax
import jax.numpy as jnp
from jax import lax
import numpy as np

D_MODEL = 4096
BATCH = 32
SEQ = 256
DEPTH = 2
DEC_BATCH = 8
DEC_SEQ = 1024
PAST_LEN = 512

GRID_W = 64
EPS = 1e-6
ROPE_THETA = 10000.0
Q_BLOCK = 128

MIX_W = D_MODEL
GROUP_W = MIX_W // 4

DN_HEADS = GROUP_W // 128
DN_DK = 128
DN_DV = 128
DN_CONV = 5
DN_CHUNK = 64

SSM_W = GROUP_W
SSM_CH = 16
SSM_GROUPS = SSM_W // SSM_CH
SSM_P = 64

GQA_HD = 128
GQA_HEADS = GROUP_W // GQA_HD
GQA_KV = 2

MLA_HEADS = GROUP_W // 128
MLA_NOPE = 128
MLA_ROPE = 64
MLA_V = 128
MLA_Q_LORA = 768
MLA_KV_LORA = 256

N_EXPERTS = 32
TOP_K = 4
D_FF = 2048
SWIGLU_LIMIT = 7.0
SWIGLU_ALPHA = 1.702
MOE_BLOCK = 128

DN_IN = 4 * GROUP_W + 4 * DN_HEADS
SSM_IN = SSM_W
GQA_IN = GQA_HEADS * GQA_HD + 2 * GQA_KV * GQA_HD
MLA_IN = MLA_Q_LORA + MLA_KV_LORA + MLA_ROPE
IN_W = DN_IN + SSM_IN + GQA_IN + MLA_IN

kernel_name = 'hymba_flow_hybrid_step'

F32 = jnp.float32


def rmsnorm(x, g):
    x32 = x.astype(F32)
    y = x32 * lax.rsqrt(jnp.mean(x32 * x32, axis=-1, keepdims=True) + EPS)
    return (y * g.astype(F32)).astype(x.dtype)


def l2norm(x):
    return x * lax.rsqrt(jnp.sum(x * x, axis=-1, keepdims=True) + EPS)


def grid_positions(n):
    rows = n // GRID_W
    row = jnp.repeat(jnp.arange(rows, dtype=F32), GRID_W)
    col = jnp.tile(jnp.arange(GRID_W, dtype=F32), rows)
    return row, col


def rope_1d(x, pos):
    half = x.shape[-1] // 2
    inv = ROPE_THETA ** (-jnp.arange(half, dtype=F32) / half)
    ang = pos[:, None] * inv[None, :]
    cos = jnp.cos(ang)[None, :, None, :]
    sin = jnp.sin(ang)[None, :, None, :]
    x32 = x.astype(F32)
    x1, x2 = x32[..., :half], x32[..., half:]
    return jnp.concatenate([x1 * cos - x2 * sin, x2 * cos + x1 * sin], axis=-1).astype(x.dtype)


def rope_2d(x, row, col):
    h = x.shape[-1] // 2
    return jnp.concatenate([rope_1d(x[..., :h], row), rope_1d(x[..., h:], col)], axis=-1)


def block_attention(q, k, v):
    b, lq, hq, dq = q.shape
    hkv = k.shape[2]
    grp = hq // hkv
    dv = v.shape[-1]
    nb = lq // Q_BLOCK
    qb = jnp.moveaxis(q.reshape(b, nb, Q_BLOCK, hkv, grp, dq), 1, 0)
    scale = dq ** -0.5

    def one_block(qblk):
        s = jnp.einsum('bqhgd,bkhd->bhgqk', qblk, k, preferred_element_type=F32) * scale
        pr = jax.nn.softmax(s, axis=-1).astype(v.dtype)
        return jnp.einsum('bhgqk,bkhd->bqhgd', pr, v)

    o = lax.map(one_block, qb)
    return jnp.moveaxis(o, 0, 1).reshape(b, lq, hq * dv)


def short_conv(x, w):
    c = x.shape[-1]
    pad = (DN_CONV - 1) // 2
    y = lax.conv_general_dilated(x, w[:, None, :].astype(x.dtype), window_strides=(1,),
                                 padding=[(pad, pad)], dimension_numbers=('NWC', 'WIO', 'NWC'),
                                 feature_group_count=c)
    return jax.nn.silu(y)


def chunk_gated_delta(q, k, v, g, beta, s0):
    b, l, h, dk = q.shape
    dv = v.shape[-1]
    n = l // DN_CHUNK

    def chunks(t):
        t = jnp.swapaxes(t, 1, 2)
        return t.reshape(t.shape[:2] + (n, DN_CHUNK) + t.shape[3:])

    q = chunks(q * dk ** -0.5)
    k = chunks(k)
    v = chunks(v)
    g = chunks(g)
    beta = chunks(beta)
    gc = jnp.cumsum(g, axis=-1)
    tri = jnp.tril(jnp.ones((DN_CHUNK, DN_CHUNK), dtype=bool))
    strict = jnp.tril(jnp.ones((DN_CHUNK, DN_CHUNK), dtype=bool), -1)
    decay = jnp.exp(jnp.where(tri, gc[..., :, None] - gc[..., None, :], -jnp.inf))
    k_beta = k * beta[..., None]
    a = jnp.where(strict, jnp.einsum('bhncd,bhnjd->bhncj', k_beta, k) * decay, 0.0)
    m = a + jnp.eye(DN_CHUNK, dtype=a.dtype)
    rhs = jnp.concatenate([v * beta[..., None], k_beta * jnp.exp(gc)[..., None]], axis=-1)
    sol = lax.linalg.triangular_solve(m, rhs, left_side=True, lower=True, unit_diagonal=True)
    u, w = sol[..., :dv], sol[..., dv:]
    qk = jnp.einsum('bhncd,bhnjd->bhncj', q, k) * decay
    q_dec = q * jnp.exp(gc)[..., None]
    k_dec = k * jnp.exp(gc[..., -1:] - gc)[..., None]
    g_last = jnp.exp(gc[..., -1])

    def step(s, inp):
        qd, kd, un, wn, qkn, gl = inp
        v_new = un - jnp.einsum('bhcd,bhde->bhce', wn, s)
        o = jnp.einsum('bhcd,bhde->bhce', qd, s) + jnp.einsum('bhcj,bhje->bhce', qkn, v_new)
        s = s * gl[..., None, None] + jnp.einsum('bhcd,bhce->bhde', kd, v_new)
        return s, o

    xs = (jnp.moveaxis(q_dec, 2, 0), jnp.moveaxis(k_dec, 2, 0), jnp.moveaxis(u, 2, 0),
          jnp.moveaxis(w, 2, 0), jnp.moveaxis(qk, 2, 0), jnp.moveaxis(g_last, 2, 0))
    s_fin, o = lax.scan(step, s0, xs)
    o = jnp.transpose(o, (1, 0, 3, 2, 4)).reshape(b, l, h, dv)
    return o, s_fin


def deltanet_mixer(xd, p, s0):
    b, l, _ = xd.shape
    qkv = short_conv(xd[..., :3 * GROUP_W], p['dn_conv']).astype(F32)
    q, k, v = [t.reshape(b, l, DN_HEADS, -1) for t in jnp.split(qkv, 3, axis=-1)]
    q = l2norm(q)
    k = l2norm(k)
    z = xd[..., 3 * GROUP_W:4 * GROUP_W].astype(F32).reshape(b, l, DN_HEADS, DN_DV)
    ab = xd[..., 4 * GROUP_W:].astype(F32).reshape(b, l, 4, DN_HEADS)
    a_log = p['dn_a_log'].astype(F32)
    dt_bias = p['dn_dt_bias'].astype(F32)
    g = -jnp.exp(a_log)[None, None] * jax.nn.softplus(ab[:, :, 0::2] + dt_bias[None, None])
    beta = jax.nn.sigmoid(ab[:, :, 1::2])
    o_f, s_f = chunk_gated_delta(q, k, v, g[:, :, 0], beta[:, :, 0], s0[:, 0])
    o_b, s_b = chunk_gated_delta(jnp.flip(q, 1), jnp.flip(k, 1), jnp.flip(v, 1),
                                 jnp.flip(g[:, :, 1], 1), jnp.flip(beta[:, :, 1], 1), s0[:, 1])
    o = rmsnorm(o_f + jnp.flip(o_b, 1), p['dn_norm']) * jax.nn.silu(z)
    return o.reshape(b, l, GROUP_W).astype(xd.dtype), jnp.stack([s_f, s_b], axis=1)


def s5_scan(bu, lbar, h0):
    bu = bu.at[:, 0].add(lbar * h0)
    a = jnp.broadcast_to(lbar, bu.shape)

    def combine(e1, e2):
        a1, b1 = e1
        a2, b2 = e2
        return a1 * a2, a2 * b1 + b2

    _, h = lax.associative_scan(combine, (a, bu), axis=1)
    return h


def s5_direction(ug, p, d, h0, reverse):
    lam = lax.complex(p['ssm_lam_re'][d].astype(F32), p['ssm_lam_im'][d].astype(F32))
    dt = jnp.exp(p['ssm_log_dt'][d].astype(F32))[:, None]
    lbar = jnp.exp(lam * dt)
    bmat = lax.complex(p['ssm_b_re'][d].astype(F32), p['ssm_b_im'][d].astype(F32))
    bbar = ((lbar - 1.0) / lam)[..., None] * bmat
    bu = jnp.einsum('blgc,gpc->blgp', ug, bbar)
    if reverse:
        h = s5_scan(jnp.flip(bu, 1), lbar, h0)
        return jnp.flip(h, 1), h[:, -1]
    h = s5_scan(bu, lbar, h0)
    return h, h[:, -1]


def s5_mixer(u, p, h0):
    b, l, _ = u.shape
    u32 = u.astype(F32).reshape(b, l, SSM_GROUPS, SSM_CH)
    ug = u32.astype(jnp.complex64)
    h_f, fin_f = s5_direction(ug, p, 0, h0[:, 0], False)
    h_b, fin_b = s5_direction(ug, p, 1, h0[:, 1], True)
    cmat = lax.complex(p['ssm_c_re'].astype(F32), p['ssm_c_im'].astype(F32))
    y = jnp.real(jnp.einsum('blgp,gcp->blgc', h_f + h_b, cmat))
    y = y + p['ssm_d'].astype(F32).reshape(SSM_GROUPS, SSM_CH) * u32
    y = jax.nn.gelu(y.reshape(b, l, SSM_W)).astype(u.dtype)
    y = y * jax.nn.sigmoid(jnp.matmul(y, p['ssm_w_glu']) + p['ssm_b_glu'])
    return y, jnp.stack([fin_f, fin_b], axis=1)


def gqa_proj(xg, p):
    b, l, _ = xg.shape
    nq = GQA_HEADS * GQA_HD
    nk = GQA_KV * GQA_HD
    q = xg[..., :nq].reshape(b, l, GQA_HEADS, GQA_HD)
    k = xg[..., nq:nq + nk].reshape(b, l, GQA_KV, GQA_HD)
    v = xg[..., nq + nk:].reshape(b, l, GQA_KV, GQA_HD)
    return rmsnorm(q, p['gqa_qn']), rmsnorm(k, p['gqa_kn']), v


def mla_proj(xm, p):
    b, l, _ = xm.shape
    qc = rmsnorm(xm[..., :MLA_Q_LORA], p['mla_qn'])
    ckv = rmsnorm(xm[..., MLA_Q_LORA:MLA_Q_LORA + MLA_KV_LORA], p['mla_kvn'])
    krope = xm[..., MLA_Q_LORA + MLA_KV_LORA:]
    q = jnp.matmul(qc, p['mla_w_uq']).reshape(b, l, MLA_HEADS, MLA_NOPE + MLA_ROPE)
    return q, ckv, krope


def mla_expand(ckv, krope, p):
    b, lk, _ = ckv.shape
    kv = jnp.matmul(ckv, p['mla_w_ukv']).reshape(b, lk, MLA_HEADS, MLA_NOPE + MLA_V)
    k = jnp.concatenate([kv[..., :MLA_NOPE],
                         jnp.broadcast_to(krope[:, :, None, :], (b, lk, MLA_HEADS, MLA_ROPE))], axis=-1)
    return k, kv[..., MLA_NOPE:]


def clamped_swiglu(gate, up):
    gate = jnp.minimum(gate, SWIGLU_LIMIT)
    up = jnp.clip(up, -SWIGLU_LIMIT, SWIGLU_LIMIT)
    return (up + 1.0) * gate * jax.nn.sigmoid(SWIGLU_ALPHA * gate)


def moe(x, p):
    b, l, d = x.shape
    xt = x.reshape(b * l, d)
    n_tok = b * l
    n_asg = n_tok * TOP_K
    logits = jnp.matmul(xt, p['moe_w_router'], preferred_element_type=F32) + p['moe_b_router'].astype(F32)
    top_val, top_idx = lax.top_k(logits, TOP_K)
    gates = jax.nn.softmax(top_val, axis=-1)
    flat_e = top_idx.reshape(-1)
    order = jnp.argsort(flat_e)
    se = flat_e[order]
    stok = (order // TOP_K).astype(jnp.int32)
    sg = gates.reshape(-1)[order]
    counts = jnp.zeros((N_EXPERTS,), jnp.int32).at[flat_e].add(1)
    starts = jnp.cumsum(counts) - counts
    pcounts = (counts + MOE_BLOCK - 1) // MOE_BLOCK * MOE_BLOCK
    pends = jnp.cumsum(pcounts)
    pstarts = pends - pcounts
    dest = pstarts[se] + jnp.arange(n_asg, dtype=jnp.int32) - starts[se]
    n_blocks = -(-(n_asg + N_EXPERTS * (MOE_BLOCK - 1)) // MOE_BLOCK)
    n_rows = n_blocks * MOE_BLOCK
    row_tok = jnp.zeros((n_rows,), jnp.int32).at[dest].set(stok)
    row_gate = jnp.zeros((n_rows,), F32).at[dest].set(sg)
    blk_e = jnp.minimum(jnp.searchsorted(pends, jnp.arange(n_blocks, dtype=jnp.int32) * MOE_BLOCK, side='right'),
                        N_EXPERTS - 1)

    def expert_block(args):
        tok, e = args
        xb = xt[tok]
        hid = clamped_swiglu(jnp.matmul(xb, p['moe_w_gate'][e]) + p['moe_b_gate'][e],
                             jnp.matmul(xb, p['moe_w_up'][e]) + p['moe_b_up'][e])
        return jnp.matmul(hid, p['moe_w_down'][e]) + p['moe_b_down'][e]

    y = lax.map(expert_block, (row_tok.reshape(n_blocks, MOE_BLOCK), blk_e))
    y = y.reshape(n_rows, d).astype(F32) * row_gate[:, None]
    out = jnp.zeros((n_tok, d), F32).at[row_tok].add(y)
    return out.astype(x.dtype).reshape(b, l, d)


def modulation(cond, p):
    m = jnp.matmul(jax.nn.silu(cond), p['w_ada']) + p['b_ada']
    return jnp.split(m[:, None, :], 6, axis=-1)


def trunk_layer(x, cond, p, ctx):
    b, l, _ = x.shape
    sh1, sc1, g1, sh2, sc2, g2 = modulation(cond, p)
    h = rmsnorm(x, p['norm1']) * (1.0 + sc1) + sh1
    xin = jnp.matmul(h, p['w_in'])
    x_dn, x_ssm, x_gqa, x_mla = jnp.split(xin, [DN_IN, DN_IN + SSM_IN, DN_IN + SSM_IN + GQA_IN], axis=-1)
    if ctx is None:
        dn0 = jnp.zeros((b, 2, DN_HEADS, DN_DK, DN_DV), F32)
        ssm0 = jnp.zeros((b, 2, SSM_GROUPS, SSM_P), jnp.complex64)
    else:
        dn0 = ctx[4].astype(F32)
        ssm0 = lax.complex(ctx[5].astype(F32), ctx[6].astype(F32))
    o_dn, dn_fin = deltanet_mixer(x_dn, p, dn0)
    o_ssm, ssm_fin = s5_mixer(x_ssm, p, ssm0)
    q, k, v = gqa_proj(x_gqa, p)
    qm, ckv, krope = mla_proj(x_mla, p)
    if ctx is None:
        o_gqa = block_attention(q, k, v)
        km, vm = mla_expand(ckv, krope, p)
        o_mla = block_attention(qm, km, vm)
        new_ctx = (k, v, ckv, krope, dn_fin, jnp.real(ssm_fin), jnp.imag(ssm_fin))
    else:
        row, col = grid_positions(l)
        q = rope_2d(q, row, col)
        k = rope_2d(k, row, col)
        o_gqa = block_attention(q, jnp.concatenate([ctx[0], k], axis=1), jnp.concatenate([ctx[1], v], axis=1))
        qm = jnp.concatenate([qm[..., :MLA_NOPE], rope_2d(qm[..., MLA_NOPE:], row, col)], axis=-1)
        krope = rope_2d(krope[:, :, None, :], row, col)[:, :, 0]
        km, vm = mla_expand(jnp.concatenate([ctx[2], ckv], axis=1), jnp.concatenate([ctx[3], krope], axis=1), p)
        o_mla = block_attention(qm, km, vm)
        new_ctx = None
    mix = jnp.matmul(jnp.concatenate([o_dn, o_ssm, o_gqa, o_mla], axis=-1), p['w_out'])
    x = x + g1 * mix
    h2 = rmsnorm(x, p['norm2']) * (1.0 + sc2) + sh2
    x = x + g2 * moe(h2, p)
    return x, new_ctx


def setup_inputs(seed: int = 0) -> dict:
    key = jax.random.key(seed)
    ks = iter(jax.random.split(key, 64))

    def nrm(shape, scale=1.0):
        return scale * jax.random.normal(next(ks), shape, F32)

    def gain(shape):
        return 1.0 + nrm(shape, 0.02)

    def unif(shape, lo, hi):
        return jax.random.uniform(next(ks), shape, F32, lo, hi)

    dt = jnp.exp(unif((DEPTH, 2, DN_HEADS), math.log(1e-3), math.log(1e-1)))
    dn_dt_bias = dt + jnp.log(-jnp.expm1(-dt))
    lam_im = jnp.pi * jnp.arange(SSM_P, dtype=F32)
    return {
        'x_prompt': nrm((BATCH, SEQ, D_MODEL)),
        'x_sample': nrm((DEC_BATCH, DEC_SEQ, D_MODEL)),
        'cache_gqa_k': nrm((DEC_BATCH, DEPTH, PAST_LEN, GQA_KV, GQA_HD)),
        'cache_gqa_v': nrm((DEC_BATCH, DEPTH, PAST_LEN, GQA_KV, GQA_HD)),
        'cache_mla_ckv': nrm((DEC_BATCH, DEPTH, PAST_LEN, MLA_KV_LORA)),
        'cache_mla_krope': nrm((DEC_BATCH, DEPTH, PAST_LEN, MLA_ROPE)),
        'state_dn': nrm((DEC_BATCH, DEPTH, 2, DN_HEADS, DN_DK, DN_DV), 0.1),
        'state_ssm_re': nrm((DEC_BATCH, DEPTH, 2, SSM_GROUPS, SSM_P), 0.5),
        'state_ssm_im': nrm((DEC_BATCH, DEPTH, 2, SSM_GROUPS, SSM_P), 0.5),
        'c': nrm((DEC_BATCH, D_MODEL)),
        'c_ctx': nrm((D_MODEL,)),
        'w_ada': nrm((DEPTH, D_MODEL, 6 * D_MODEL), 0.5 * D_MODEL ** -0.5),
        'b_ada': nrm((DEPTH, 6 * D_MODEL), 0.01),
        'norm1_g': gain((DEPTH, D_MODEL)),
        'norm2_g': gain((DEPTH, D_MODEL)),
        'w_in': nrm((DEPTH, D_MODEL, IN_W), D_MODEL ** -0.5),
        'dn_conv': nrm((DEPTH, DN_CONV, 3 * GROUP_W), DN_CONV ** -0.5),
        'dn_a_log': jnp.log(unif((DEPTH, 2, DN_HEADS), 1.0, 16.0)),
        'dn_dt_bias': dn_dt_bias,
        'dn_norm_g': gain((DEPTH, DN_DV)),
        'ssm_lam_re': -0.5 + nrm((DEPTH, 2, SSM_GROUPS, SSM_P), 0.01),
        'ssm_lam_im': lam_im + nrm((DEPTH, 2, SSM_GROUPS, SSM_P), 0.01),
        'ssm_log_dt': unif((DEPTH, 2, SSM_GROUPS), math.log(1e-3), math.log(1e-1)),
        'ssm_b_re': nrm((DEPTH, 2, SSM_GROUPS, SSM_P, SSM_CH), (2 * SSM_CH) ** -0.5),
        'ssm_b_im': nrm((DEPTH, 2, SSM_GROUPS, SSM_P, SSM_CH), (2 * SSM_CH) ** -0.5),
        'ssm_c_re': nrm((DEPTH, SSM_GROUPS, SSM_CH, SSM_P), (2 * SSM_P) ** -0.5),
        'ssm_c_im': nrm((DEPTH, SSM_GROUPS, SSM_CH, SSM_P), (2 * SSM_P) ** -0.5),
        'ssm_d': nrm((DEPTH, SSM_W)),
        'ssm_w_glu': nrm((DEPTH, SSM_W, SSM_W), SSM_W ** -0.5),
        'ssm_b_glu': nrm((DEPTH, SSM_W), 0.01),
        'gqa_q_norm': gain((DEPTH, GQA_HD)),
        'gqa_k_norm': gain((DEPTH, GQA_HD)),
        'mla_q_norm': gain((DEPTH, MLA_Q_LORA)),
        'mla_kv_norm': gain((DEPTH, MLA_KV_LORA)),
        'mla_w_uq': nrm((DEPTH, MLA_Q_LORA, MLA_HEADS * (MLA_NOPE + MLA_ROPE)), MLA_Q_LORA ** -0.5),
        'mla_w_ukv': nrm((DEPTH, MLA_KV_LORA, MLA_HEADS * (MLA_NOPE + MLA_V)), MLA_KV_LORA ** -0.5),
        'w_out': nrm((DEPTH, MIX_W, D_MODEL), MIX_W ** -0.5),
        'moe_w_router': nrm((DEPTH, D_MODEL, N_EXPERTS), D_MODEL ** -0.5),
        'moe_b_router': nrm((DEPTH, N_EXPERTS), 0.01),
        'moe_w_gate': nrm((DEPTH, N_EXPERTS, D_MODEL, D_FF), D_MODEL ** -0.5),
        'moe_b_gate': nrm((DEPTH, N_EXPERTS, D_FF), 0.01),
        'moe_w_up': nrm((DEPTH, N_EXPERTS, D_MODEL, D_FF), D_MODEL ** -0.5),
        'moe_b_up': nrm((DEPTH, N_EXPERTS, D_FF), 0.01),
        'moe_w_down': nrm((DEPTH, N_EXPERTS, D_FF, D_MODEL), D_FF ** -0.5),
        'moe_b_down': nrm((DEPTH, N_EXPERTS, D_MODEL), 0.01),
        'norm_f': gain((D_MODEL,)),
    }


def reference(x_prompt, x_sample, cache_gqa_k, cache_gqa_v, cache_mla_ckv, cache_mla_krope, state_dn,
              state_ssm_re, state_ssm_im, c, c_ctx, w_ada, b_ada, norm1_g, norm2_g, w_in, dn_conv, dn_a_log,
              dn_dt_bias, dn_norm_g, ssm_lam_re, ssm_lam_im, ssm_log_dt, ssm_b_re, ssm_b_im, ssm_c_re, ssm_c_im,
              ssm_d, ssm_w_glu, ssm_b_glu, gqa_q_norm, gqa_k_norm, mla_q_norm, mla_kv_norm, mla_w_uq, mla_w_ukv,
              w_out, moe_w_router, moe_b_router, moe_w_gate, moe_b_gate, moe_w_up, moe_b_up, moe_w_down,
              moe_b_down, norm_f):
    x_ctx = x_prompt
    x_lat = x_sample
    cond_ctx = c_ctx[None, :]
    ks, vs, ckvs, kropes, dns, ssm_res, ssm_ims = [], [], [], [], [], [], []
    for l in range(DEPTH):
        p = {
            'w_ada': w_ada[l], 'b_ada': b_ada[l], 'norm1': norm1_g[l], 'norm2': norm2_g[l], 'w_in': w_in[l],
            'dn_conv': dn_conv[l], 'dn_a_log': dn_a_log[l], 'dn_dt_bias': dn_dt_bias[l], 'dn_norm': dn_norm_g[l],
            'ssm_lam_re': ssm_lam_re[l], 'ssm_lam_im': ssm_lam_im[l], 'ssm_log_dt': ssm_log_dt[l],
            'ssm_b_re': ssm_b_re[l], 'ssm_b_im': ssm_b_im[l], 'ssm_c_re': ssm_c_re[l], 'ssm_c_im': ssm_c_im[l],
            'ssm_d': ssm_d[l], 'ssm_w_glu': ssm_w_glu[l], 'ssm_b_glu': ssm_b_glu[l],
            'gqa_qn': gqa_q_norm[l], 'gqa_kn': gqa_k_norm[l], 'mla_qn': mla_q_norm[l], 'mla_kvn': mla_kv_norm[l],
            'mla_w_uq': mla_w_uq[l], 'mla_w_ukv': mla_w_ukv[l], 'w_out': w_out[l],
            'moe_w_router': moe_w_router[l], 'moe_b_router': moe_b_router[l], 'moe_w_gate': moe_w_gate[l],
            'moe_b_gate': moe_b_gate[l], 'moe_w_up': moe_w_up[l], 'moe_b_up': moe_b_up[l],
            'moe_w_down': moe_w_down[l], 'moe_b_down': moe_b_down[l],
        }
        x_ctx, nc = trunk_layer(x_ctx, cond_ctx, p, None)
        ks.append(nc[0])
        vs.append(nc[1])
        ckvs.append(nc[2])
        kropes.append(nc[3])
        dns.append(nc[4])
        ssm_res.append(nc[5])
        ssm_ims.append(nc[6])
        cached = (cache_gqa_k[:, l], cache_gqa_v[:, l], cache_mla_ckv[:, l], cache_mla_krope[:, l],
                  state_dn[:, l], state_ssm_re[:, l], state_ssm_im[:, l])
        x_lat, _ = trunk_layer(x_lat, c, p, cached)
    y_prompt = rmsnorm(x_ctx, norm_f)
    y_sample = rmsnorm(x_lat, norm_f)
    new_gqa_k = jnp.stack(ks, axis=1)
    new_gqa_v = jnp.stack(vs, axis=1)
    new_mla_ckv = jnp.stack(ckvs, axis=1)
    new_mla_krope = jnp.stack(kropes, axis=1)
    new_dn_state = jnp.stack(dns, axis=1)
    new_ssm_re = jnp.stack(ssm_res, axis=1)
    new_ssm_im = jnp.stack(ssm_ims, axis=1)
    return (y_prompt, y_sample, new_gqa_k, new_gqa_v, new_mla_ckv, new_mla_krope, new_dn_state, new_ssm_re, new_ssm_im)
```

```python
import functools
import math

import jax
import jax.numpy as jnp
from jax import lax
from jax.experimental import pallas as pl
from jax.experimental.pallas import tpu as pltpu

F32 = jnp.float32
BF16 = jnp.bfloat16
I32 = jnp.int32

EPS = 1e-6
ROPE_THETA = 10000.0
GRID_W = 64
N_EXPERTS = 32
TOP_K = 4
SWIGLU_LIMIT = 7.0
SWIGLU_ALPHA = 1.702
DN_CONV = 5
DN_CHUNK = 64
DN_HEAD_DIM = 128
SSM_CH = 16
SSM_P = 64
SSM_T = 16
GQA_HD = 128
GQA_KV = 2
MLA_NOPE = 128
MLA_ROPE = 64
MLA_V = 128
MLA_Q_LORA = 768
MLA_KV_LORA = 256

LANES = 128
VMEM_LIMIT = 56 * 1024 * 1024
NEG_BIG = -1e30
HIGHEST = lax.Precision.HIGHEST


def _cparams(n_axes):
    return pltpu.CompilerParams(dimension_semantics=("arbitrary",) * n_axes, vmem_limit_bytes=VMEM_LIMIT)


def _tile(n, want):
    t = min(n, want)
    while n % t:
        t //= 2
    return t


def _bdot(a, b):
    return jnp.dot(a.astype(BF16), b.astype(BF16), preferred_element_type=F32)


def _bdot_nt(a, b):
    return lax.dot_general(a.astype(BF16), b.astype(BF16), (((1,), (1,)), ((), ())), preferred_element_type=F32)


def _bdot_tn(a, b):
    return lax.dot_general(a.astype(BF16), b.astype(BF16), (((0,), (0,)), ((), ())), preferred_element_type=F32)


def _split(a):
    hi = a.astype(BF16)
    lo = (a - hi.astype(F32)).astype(BF16)
    return hi, lo


def _dot3(a, b):
    ah, al = _split(a)
    bh, bl = _split(b)
    d = functools.partial(jnp.dot, preferred_element_type=F32)
    return d(ah, bh) + (d(ah, bl) + d(al, bh))


def _sigmoid(x):
    return 1.0 / (1.0 + jnp.exp(-x))


def _silu(x):
    return x * _sigmoid(x)


def _rms(x, g):
    return x * lax.rsqrt(jnp.mean(x * x, axis=-1, keepdims=True) + EPS) * g


def _ada_kernel(c_ref, w_ref, b_ref, o_ref):
    o_ref[...] = _bdot(_silu(c_ref[...]), w_ref[...]) + b_ref[...]


def _ada(cond, w, b):
    r, d = cond.shape
    n = w.shape[1]
    tn = _tile(n, 512)
    return pl.pallas_call(
        _ada_kernel, name="ada",
        out_shape=jax.ShapeDtypeStruct((r, n), F32),
        grid=(n // tn,),
        in_specs=[pl.BlockSpec((r, d), lambda j: (0, 0)),
                  pl.BlockSpec((d, tn), lambda j: (0, j)),
                  pl.BlockSpec((1, tn), lambda j: (0, j))],
        out_specs=pl.BlockSpec((r, tn), lambda j: (0, j)),
        compiler_params=_cparams(1),
    )(cond, w, b.reshape(1, n))


def _mod_row(i, tm, m_ctx, l_lat):
    r0 = i * tm
    return jnp.where(r0 < m_ctx, 0, 1 + (r0 - m_ctx) // l_lat)


def _mod_spec(d, which, tm, m_ctx, l_lat, n_grid):
    if n_grid == 1:
        return pl.BlockSpec((None, None, 1, d), lambda i: (_mod_row(i, tm, m_ctx, l_lat), which, 0, 0))
    return pl.BlockSpec((None, None, 1, d), lambda i, j: (_mod_row(i, tm, m_ctx, l_lat), which, 0, 0))


def _inproj_kernel(x_ref, sh_ref, sc_ref, g_ref, w_ref, o_ref, h_ref):
    @pl.when(pl.program_id(1) == 0)
    def _():
        y = _rms(x_ref[...], g_ref[...])
        h_ref[...] = (y * (1.0 + sc_ref[...]) + sh_ref[...]).astype(BF16)

    o_ref[...] = jnp.dot(h_ref[...], w_ref[...], preferred_element_type=F32)


def _inproj(x, mod4, g, w, m_ctx, l_lat):
    m, d = x.shape
    n = w.shape[1]
    tm = _tile(math.gcd(m_ctx, l_lat), 512)
    tn = _tile(n, 512)
    if n % tn or tn % LANES:
        tn = LANES
    return pl.pallas_call(
        _inproj_kernel, name="inproj",
        out_shape=jax.ShapeDtypeStruct((m, n), F32),
        grid=(m // tm, n // tn),
        in_specs=[pl.BlockSpec((tm, d), lambda i, j: (i, 0)),
                  _mod_spec(d, 0, tm, m_ctx, l_lat, 2),
                  _mod_spec(d, 1, tm, m_ctx, l_lat, 2),
                  pl.BlockSpec((1, d), lambda i, j: (0, 0)),
                  pl.BlockSpec((d, tn), lambda i, j: (0, j))],
        out_specs=pl.BlockSpec((tm, tn), lambda i, j: (i, j)),
        scratch_shapes=[pltpu.VMEM((tm, d), BF16)],
        compiler_params=_cparams(2),
    )(x, mod4, mod4, g.reshape(1, d), w)


def _outproj_kernel(a0, a1, a2, a3, w_ref, x_ref, g_ref, o_ref):
    gw = a0.shape[1]
    acc = jnp.dot(a0[...], w_ref[0:gw, :], preferred_element_type=F32)
    acc += jnp.dot(a1[...], w_ref[gw:2 * gw, :], preferred_element_type=F32)
    acc += jnp.dot(a2[...], w_ref[2 * gw:3 * gw, :], preferred_element_type=F32)
    acc += jnp.dot(a3[...], w_ref[3 * gw:4 * gw, :], preferred_element_type=F32)
    o_ref[...] = x_ref[...] + g_ref[...] * acc


def _outproj(parts, w, x, mod4, m_ctx, l_lat):
    m, d = x.shape
    gw = parts[0].shape[1]
    tm = _tile(math.gcd(m_ctx, l_lat), 512)
    tn = _tile(d, 512)
    part_spec = pl.BlockSpec((tm, gw), lambda i, j: (i, 0))
    g_spec = pl.BlockSpec((None, None, 1, tn), lambda i, j: (_mod_row(i, tm, m_ctx, l_lat), 2, 0, j))
    return pl.pallas_call(
        _outproj_kernel, name="outproj",
        out_shape=jax.ShapeDtypeStruct((m, d), F32),
        grid=(m // tm, d // tn),
        in_specs=[part_spec] * 4 + [pl.BlockSpec((4 * gw, tn), lambda i, j: (0, j)),
                                    pl.BlockSpec((tm, tn), lambda i, j: (i, j)),
                                    g_spec],
        out_specs=pl.BlockSpec((tm, tn), lambda i, j: (i, j)),
        compiler_params=_cparams(2),
    )(*parts, w, x, mod4)


def _router_kernel(x_ref, sh_ref, sc_ref, g_ref, wr_ref, br_ref, h_ref, idx_ref, gate_ref):
    y = _rms(x_ref[...], g_ref[...])
    h = y * (1.0 + sc_ref[...]) + sh_ref[...]
    h_ref[...] = h
    logits = jnp.dot(h, wr_ref[...], precision=HIGHEST, preferred_element_type=F32) + br_ref[...]
    lane = lax.broadcasted_iota(I32, logits.shape, 1).astype(F32)
    cur = logits
    vals, idxs = [], []
    for _ in range(TOP_K):
        mx = jnp.max(cur, axis=-1, keepdims=True)
        am = jnp.min(jnp.where(cur == mx, lane, float(LANES)), axis=-1, keepdims=True)
        vals.append(mx)
        idxs.append(am)
        cur = jnp.where(lane == am, -jnp.inf, cur)
    es = [jnp.exp(v - vals[0]) for v in vals]
    tot = es[0] + es[1] + es[2] + es[3]
    idx_o = jnp.zeros(logits.shape, F32)
    gate_o = jnp.zeros(logits.shape, F32)
    for k in range(TOP_K):
        idx_o = jnp.where(lane == float(k), idxs[k], idx_o)
        gate_o = jnp.where(lane == float(k), es[k] / tot, gate_o)
    idx_ref[...] = idx_o.astype(I32)
    gate_ref[...] = gate_o


def _router(x, mod4, g, wr, br, m_ctx, l_lat):
    m, d = x.shape
    tm = _tile(math.gcd(m_ctx, l_lat), 256)
    wr_p = jnp.zeros((d, LANES), F32).at[:, :N_EXPERTS].set(wr)
    br_p = jnp.full((1, LANES), NEG_BIG, F32).at[0, :N_EXPERTS].set(br)
    return pl.pallas_call(
        _router_kernel, name="router",
        out_shape=(jax.ShapeDtypeStruct((m, d), F32),
                   jax.ShapeDtypeStruct((m, LANES), I32),
                   jax.ShapeDtypeStruct((m, LANES), F32)),
        grid=(m // tm,),
        in_specs=[pl.BlockSpec((tm, d), lambda i: (i, 0)),
                  _mod_spec(d, 3, tm, m_ctx, l_lat, 1),
                  _mod_spec(d, 4, tm, m_ctx, l_lat, 1),
                  pl.BlockSpec((1, d), lambda i: (0, 0)),
                  pl.BlockSpec((d, LANES), lambda i: (0, 0)),
                  pl.BlockSpec((1, LANES), lambda i: (0, 0))],
        out_specs=(pl.BlockSpec((tm, d), lambda i: (i, 0)),
                   pl.BlockSpec((tm, LANES), lambda i: (i, 0)),
                   pl.BlockSpec((tm, LANES), lambda i: (i, 0))),
        compiler_params=_cparams(1),
    )(x, mod4, mod4, g.reshape(1, d), wr_p, br_p)


def _route_plan(top_idx, tm, n_tiles):
    m = top_idx.shape[0]
    flat_e = top_idx.reshape(-1)
    onehot = (flat_e[:, None] == jnp.arange(N_EXPERTS, dtype=I32)[None, :]).astype(I32)
    csum = jnp.cumsum(onehot, axis=0)
    rank = jnp.take_along_axis(csum, flat_e[:, None], axis=1)[:, 0] - 1
    counts = csum[-1]
    tiles_e = (counts + tm - 1) // tm
    tile_end = jnp.cumsum(tiles_e)
    tile_start = tile_end - tiles_e
    dest = tile_start[flat_e] * tm + rank
    row_tok = jnp.zeros((n_tiles * tm,), I32).at[dest].set(jnp.arange(m * TOP_K, dtype=I32) // TOP_K)
    return dest.astype(I32), row_tok, tiles_e.astype(I32), tile_start.astype(I32), tile_end.astype(I32)


def _schedule(tiles_e, tile_start, tile_end, n_tiles, nj):
    n_used = tile_end[-1]
    step = jnp.arange(n_tiles * nj, dtype=I32)
    s = jnp.minimum(step, n_used * nj - 1)
    e = jnp.minimum(jnp.searchsorted(tile_end * nj, s, side="right").astype(I32), N_EXPERTS - 1)
    local = s - tile_start[e] * nj
    t_e = jnp.maximum(tiles_e[e], 1)
    j = (local // t_e).astype(I32)
    r = local - j * t_e
    tile = (tile_start[e] + r).astype(I32)
    valid = step < n_used * nj
    spare = step - n_used * nj
    out_tile = jnp.where(valid, tile, n_used + spare // nj).astype(I32)
    out_j = jnp.where(valid, j, spare % nj).astype(I32)
    flags = jnp.where(valid, 1 + 2 * (r == 0).astype(I32), 4).astype(I32)
    return e, j, tile, out_tile, out_j, flags


def _gather_kernel(tok_ref, nused_ref, h_hbm, o_ref, buf, sem):
    i = pl.program_id(0)
    tg = buf.shape[0]

    @pl.when(i < nused_ref[0])
    def _():
        def issue(r, c):
            t = tok_ref[i * tg + r]
            pltpu.make_async_copy(h_hbm.at[pl.ds(t, 1)], buf.at[pl.ds(r, 1)], sem).start()
            return c

        lax.fori_loop(0, tg, issue, 0)

        def wait(r, c):
            pltpu.make_async_copy(h_hbm.at[pl.ds(0, 1)], buf.at[pl.ds(r, 1)], sem).wait()
            return c

        lax.fori_loop(0, tg, wait, 0)
        o_ref[...] = buf[...].astype(BF16)

    @pl.when(i >= nused_ref[0])
    def _():
        o_ref[...] = jnp.zeros(o_ref.shape, o_ref.dtype)


def _dispatch(h, row_tok, n_used_rows_tiles, tg):
    m, d = h.shape
    n_rows = row_tok.shape[0]
    grid_spec = pltpu.PrefetchScalarGridSpec(
        num_scalar_prefetch=2, grid=(n_rows // tg,),
        in_specs=[pl.BlockSpec(memory_space=pl.ANY)],
        out_specs=pl.BlockSpec((tg, d), lambda i, tok, nu: (i, 0)),
        scratch_shapes=[pltpu.VMEM((tg, d), F32), pltpu.SemaphoreType.DMA(())])
    return pl.pallas_call(
        _gather_kernel, name="dispatch", grid_spec=grid_spec,
        out_shape=jax.ShapeDtypeStruct((n_rows, d), BF16),
        compiler_params=_cparams(1),
    )(row_tok, n_used_rows_tiles, h)


def _moe_up_kernel(se, sj, st, so, soj, sf, x_ref, wg_ref, wu_ref, bg_ref, bu_ref, o_ref, wgb, wub):
    flag = sf[pl.program_id(0)]

    @pl.when((flag & 4) != 0)
    def _():
        o_ref[...] = jnp.zeros(o_ref.shape, o_ref.dtype)

    @pl.when((flag & 2) != 0)
    def _():
        wgb[...] = wg_ref[...].astype(BF16)
        wub[...] = wu_ref[...].astype(BF16)

    @pl.when((flag & 1) != 0)
    def _():
        x = x_ref[...]
        gate = jnp.dot(x, wgb[...], preferred_element_type=F32) + bg_ref[...]
        up = jnp.dot(x, wub[...], preferred_element_type=F32) + bu_ref[...]
        gate = jnp.minimum(gate, SWIGLU_LIMIT)
        up = jnp.clip(up, -SWIGLU_LIMIT, SWIGLU_LIMIT)
        o_ref[...] = ((up + 1.0) * gate * _sigmoid(SWIGLU_ALPHA * gate)).astype(BF16)


def _moe_up(xs, wg, wu, bg, bu, sched, tm, tn):
    n_rows, d = xs.shape
    e, _, f = wg.shape
    n_steps = sched[0].shape[0]
    w_spec = pl.BlockSpec((None, d, tn), lambda s, se, sj, st, so, soj, sf: (se[s], 0, sj[s]))
    b_spec = pl.BlockSpec((None, 1, tn), lambda s, se, sj, st, so, soj, sf: (se[s], 0, sj[s]))
    grid_spec = pltpu.PrefetchScalarGridSpec(
        num_scalar_prefetch=6, grid=(n_steps,),
        in_specs=[pl.BlockSpec((tm, d), lambda s, se, sj, st, so, soj, sf: (st[s], 0)),
                  w_spec, w_spec, b_spec, b_spec],
        out_specs=pl.BlockSpec((tm, tn), lambda s, se, sj, st, so, soj, sf: (so[s], soj[s])),
        scratch_shapes=[pltpu.VMEM((d, tn), BF16), pltpu.VMEM((d, tn), BF16)])
    return pl.pallas_call(
        _moe_up_kernel, name="moe_up", grid_spec=grid_spec,
        out_shape=jax.ShapeDtypeStruct((n_rows, f), BF16),
        compiler_params=_cparams(1),
    )(*sched, xs, wg, wu, bg.reshape(e, 1, f), bu.reshape(e, 1, f))


def _moe_down_kernel(se, sj, st, so, soj, sf, h_ref, wd_ref, bd_ref, o_ref, wdb):
    flag = sf[pl.program_id(0)]

    @pl.when((flag & 4) != 0)
    def _():
        o_ref[...] = jnp.zeros(o_ref.shape, o_ref.dtype)

    @pl.when((flag & 2) != 0)
    def _():
        wdb[...] = wd_ref[...].astype(BF16)

    @pl.when((flag & 1) != 0)
    def _():
        o_ref[...] = jnp.dot(h_ref[...], wdb[...], preferred_element_type=F32) + bd_ref[...]


def _moe_down(hid, wd, bd, sched, tm, tn):
    n_rows, f = hid.shape
    e, _, d = wd.shape
    n_steps = sched[0].shape[0]
    grid_spec = pltpu.PrefetchScalarGridSpec(
        num_scalar_prefetch=6, grid=(n_steps,),
        in_specs=[pl.BlockSpec((tm, f), lambda s, se, sj, st, so, soj, sf: (st[s], 0)),
                  pl.BlockSpec((None, f, tn), lambda s, se, sj, st, so, soj, sf: (se[s], 0, sj[s])),
                  pl.BlockSpec((None, 1, tn), lambda s, se, sj, st, so, soj, sf: (se[s], 0, sj[s]))],
        out_specs=pl.BlockSpec((tm, tn), lambda s, se, sj, st, so, soj, sf: (so[s], soj[s])),
        scratch_shapes=[pltpu.VMEM((f, tn), BF16)])
    return pl.pallas_call(
        _moe_down_kernel, name="moe_down", grid_spec=grid_spec,
        out_shape=jax.ShapeDtypeStruct((n_rows, d), F32),
        compiler_params=_cparams(1),
    )(*sched, hid, wd, bd.reshape(e, 1, d))


def _combine_kernel(pos_ref, y_hbm, x_ref, gate_ref, g2_ref, nf_ref, o_ref, buf, sem, *, final_norm):
    i = pl.program_id(0)
    tc = x_ref.shape[0]

    def issue(t, c):
        for k in range(TOP_K):
            p = pos_ref[(i * tc + t) * TOP_K + k]
            pltpu.make_async_copy(y_hbm.at[pl.ds(p, 1)], buf.at[k, pl.ds(t, 1)], sem).start()
        return c

    lax.fori_loop(0, tc, issue, 0)

    def wait(t, c):
        for k in range(TOP_K):
            pltpu.make_async_copy(y_hbm.at[pl.ds(0, 1)], buf.at[k, pl.ds(t, 1)], sem).wait()
        return c

    lax.fori_loop(0, tc, wait, 0)
    gates = gate_ref[...]
    acc = gates[:, 0:1] * buf[0]
    for k in range(1, TOP_K):
        acc += gates[:, k:k + 1] * buf[k]
    out = x_ref[...] + g2_ref[...] * acc
    if final_norm:
        out = _rms(out, nf_ref[...])
    o_ref[...] = out


def _combine(y, pos, x, gates, mod4, norm_f, m_ctx, l_lat, final_norm):
    m, d = x.shape
    tc = _tile(math.gcd(m_ctx, l_lat), 128)
    grid_spec = pltpu.PrefetchScalarGridSpec(
        num_scalar_prefetch=1, grid=(m // tc,),
        in_specs=[pl.BlockSpec(memory_space=pl.ANY),
                  pl.BlockSpec((tc, d), lambda i, pos: (i, 0)),
                  pl.BlockSpec((tc, LANES), lambda i, pos: (i, 0)),
                  pl.BlockSpec((None, None, 1, d), lambda i, pos: (_mod_row(i, tc, m_ctx, l_lat), 5, 0, 0)),
                  pl.BlockSpec((1, d), lambda i, pos: (0, 0))],
        out_specs=pl.BlockSpec((tc, d), lambda i, pos: (i, 0)),
        scratch_shapes=[pltpu.VMEM((TOP_K, tc, d), F32), pltpu.SemaphoreType.DMA(())])
    return pl.pallas_call(
        functools.partial(_combine_kernel, final_norm=final_norm), name="combine", grid_spec=grid_spec,
        out_shape=jax.ShapeDtypeStruct((m, d), F32),
        compiler_params=_cparams(1),
    )(pos, y, x, gates, mod4, norm_f.reshape(1, d))


def _moe(x, mod4, norm2_g, wr, br, wg, bg, wu, bu, wd, bd, norm_f, m_ctx, l_lat, final_norm):
    m, d = x.shape
    f = wg.shape[2]
    tm = _tile(m, 512)
    n_tiles = -(-(m * TOP_K + N_EXPERTS * (tm - 1)) // tm)
    h, top_idx, gates = _router(x, mod4, norm2_g, wr, br, m_ctx, l_lat)
    dest, row_tok, tiles_e, tile_start, tile_end = _route_plan(top_idx[:, :TOP_K], tm, n_tiles)
    xs = _dispatch(h, row_tok, tile_end[-1:], tm)
    tn_up = _tile(f, 256)
    sched_up = _schedule(tiles_e, tile_start, tile_end, n_tiles, f // tn_up)
    hid = _moe_up(xs, wg, wu, bg, bu, sched_up, tm, tn_up)
    tn_dn = _tile(d, 512)
    sched_dn = _schedule(tiles_e, tile_start, tile_end, n_tiles, d // tn_dn)
    y = _moe_down(hid, wd, bd, sched_dn, tm, tn_dn)
    return _combine(y, dest, x, gates, mod4, norm_f, m_ctx, l_lat, final_norm)


def _rope_tables(l, d):
    t = jnp.arange(l, dtype=I32)
    row = (t // GRID_W).astype(F32)[:, None]
    col = (t % GRID_W).astype(F32)[:, None]
    q = d // 4
    lane = jnp.arange(LANES, dtype=I32) % d
    pos = jnp.where((lane // (2 * q))[None, :] == 0, row, col)
    inv = ROPE_THETA ** (-(lane % q).astype(F32) / q)
    ang = pos * inv[None, :]
    sign = jnp.where((lane % (2 * q)) < q, -1.0, 1.0)[None, :]
    return jnp.cos(ang), jnp.sin(ang) * sign


def _rope(x, cos, sin, d):
    q = d // 4
    lane = lax.broadcasted_iota(I32, x.shape, 1)
    fwd = pltpu.roll(x, LANES - q, 1)
    bwd = pltpu.roll(x, q, 1)
    return x * cos + jnp.where((lane % (2 * q)) < q, fwd, bwd) * sin


def _attend(q_scr, k_scr, v_scr, o_ref, col0, tq):
    dv = v_scr.shape[1]

    def body(i, c):
        r0 = pl.multiple_of(i * tq, tq)
        s = lax.dot_general(q_scr[pl.ds(r0, tq), :], k_scr[...], (((1,), (1,)), ((), ())),
                            preferred_element_type=F32)
        e = jnp.exp(s - jnp.max(s, axis=-1, keepdims=True))
        o = jnp.dot(e.astype(BF16), v_scr[...], preferred_element_type=F32)
        o = o * (1.0 / jnp.sum(e, axis=-1, keepdims=True))
        o_ref[pl.ds(r0, tq), col0:col0 + dv] = o.astype(o_ref.dtype)
        return c

    lax.fori_loop(0, q_scr.shape[0] // tq, body, 0)


def _gqa_kernel(*refs, has_cache, grp):
    if has_cache:
        q_ref, k_ref, v_ref, qn_ref, kn_ref, ck_ref, cv_ref, cos_ref, sin_ref, o_ref, q_scr, k_scr, v_scr = refs
    else:
        q_ref, k_ref, v_ref, qn_ref, kn_ref, o_ref, kout_ref, q_scr, k_scr, v_scr = refs
    l = k_ref.shape[0]
    k = _rms(k_ref[...], kn_ref[...])
    if has_cache:
        p = ck_ref.shape[0]
        k_scr[0:p, :] = ck_ref[...].astype(BF16)
        v_scr[0:p, :] = cv_ref[...].astype(BF16)
        k_scr[p:p + l, :] = _rope(k, cos_ref[...], sin_ref[...], GQA_HD).astype(BF16)
        v_scr[p:p + l, :] = v_ref[...].astype(BF16)
    else:
        kout_ref[...] = k
        k_scr[...] = k.astype(BF16)
        v_scr[...] = v_ref[...].astype(BF16)
    scale = GQA_HD ** -0.5
    for g in range(grp):
        q = _rms(q_ref[:, g * GQA_HD:(g + 1) * GQA_HD], qn_ref[...])
        if has_cache:
            q = _rope(q, cos_ref[...], sin_ref[...], GQA_HD)
        q_scr[...] = (q * scale).astype(BF16)
        _attend(q_scr, k_scr, v_scr, o_ref, g * GQA_HD, _tile(l, 256))


def _gqa(xin, cols, qn, kn, b, l, row_blk0, cache=None, layer=0):
    q0, k0, v0 = cols
    n_q = k0 - q0
    grp = n_q // GQA_KV // GQA_HD
    qw = grp * GQA_HD
    has_cache = cache is not None
    in_specs = [pl.BlockSpec((l, qw), lambda i, h: (row_blk0 + i, q0 // qw + h)),
                pl.BlockSpec((l, GQA_HD), lambda i, h: (row_blk0 + i, k0 // GQA_HD + h)),
                pl.BlockSpec((l, GQA_HD), lambda i, h: (row_blk0 + i, v0 // GQA_HD + h)),
                pl.BlockSpec((1, GQA_HD), lambda i, h: (0, 0)),
                pl.BlockSpec((1, GQA_HD), lambda i, h: (0, 0))]
    args = [xin, xin, xin, qn.reshape(1, GQA_HD), kn.reshape(1, GQA_HD)]
    lk = l
    o_shape = jax.ShapeDtypeStruct((b * l, n_q), BF16)
    o_spec = pl.BlockSpec((l, qw), lambda i, h: (i, h))
    if has_cache:
        ck, cv = cache
        p = ck.shape[2]
        lk = l + p
        cspec = pl.BlockSpec((None, None, p, GQA_HD), lambda i, h: (i, layer, 0, h))
        tspec = pl.BlockSpec((l, LANES), lambda i, h: (0, 0))
        cos, sin = _rope_tables(l, GQA_HD)
        in_specs += [cspec, cspec, tspec, tspec]
        args += [ck, cv, cos, sin]
        out_shape, out_specs = o_shape, o_spec
    else:
        out_shape = (o_shape, jax.ShapeDtypeStruct((b * l, GQA_KV * GQA_HD), F32))
        out_specs = (o_spec, pl.BlockSpec((l, GQA_HD), lambda i, h: (i, h)))
    return pl.pallas_call(
        functools.partial(_gqa_kernel, has_cache=has_cache, grp=grp), name="gqa_lat" if has_cache else "gqa_ctx",
        out_shape=out_shape, grid=(b, GQA_KV), in_specs=in_specs, out_specs=out_specs,
        scratch_shapes=[pltpu.VMEM((l, GQA_HD), BF16), pltpu.VMEM((lk, GQA_HD), BF16), pltpu.VMEM((lk, GQA_HD), BF16)],
        compiler_params=_cparams(2),
    )(*args)


def _mla_kernel(*refs, has_cache, heads):
    if has_cache:
        (qc_ref, ckv_ref, kr_ref, qn_ref, kvn_ref, wqn_ref, wqr_ref, wuk_ref, wuv_ref, cc_ref, ckr_ref, cos_ref,
         sin_ref, o_ref, qn_scr, qr_scr, kn_scr, vv_scr, kr_scr) = refs
    else:
        (qc_ref, ckv_ref, kr_ref, qn_ref, kvn_ref, wqn_ref, wqr_ref, wuk_ref, wuv_ref, o_ref, ckv_out_ref,
         qn_scr, qr_scr, kn_scr, vv_scr, kr_scr) = refs
    l = qc_ref.shape[0]
    scale = (MLA_NOPE + MLA_ROPE) ** -0.5
    qc = _rms(qc_ref[...], qn_ref[...]).astype(BF16)
    qn_scr[...] = jnp.dot(qc, wqn_ref[...], preferred_element_type=F32) * scale
    qr = jnp.dot(qc, wqr_ref[...], preferred_element_type=F32) * scale
    ckv = _rms(ckv_ref[...], kvn_ref[...])
    kr = kr_ref[...]
    lane = lax.broadcasted_iota(I32, kr.shape, 1)
    kr = jnp.where(lane < MLA_ROPE, kr, pltpu.roll(kr, MLA_ROPE, 1))
    if has_cache:
        p = cc_ref.shape[0]
        cos, sin = cos_ref[...], sin_ref[...]
        for c in range(qr.shape[1] // LANES):
            qr_scr[:, c * LANES:(c + 1) * LANES] = _rope(qr[:, c * LANES:(c + 1) * LANES], cos, sin, MLA_ROPE)
        kr_scr[0:p, :] = ckr_ref[...].astype(BF16)
        kr_scr[p:p + l, :] = _rope(kr, cos, sin, MLA_ROPE).astype(BF16)
        cc = cc_ref[...].astype(BF16)
        kn_scr[0:p, :] = jnp.dot(cc, wuk_ref[...], preferred_element_type=F32).astype(BF16)
        vv_scr[0:p, :] = jnp.dot(cc, wuv_ref[...], preferred_element_type=F32).astype(BF16)
    else:
        p = 0
        qr_scr[...] = qr
        kr_scr[...] = kr.astype(BF16)
        ckv_out_ref[...] = ckv
    cb = ckv.astype(BF16)
    kn_scr[p:p + l, :] = jnp.dot(cb, wuk_ref[...], preferred_element_type=F32).astype(BF16)
    vv_scr[p:p + l, :] = jnp.dot(cb, wuv_ref[...], preferred_element_type=F32).astype(BF16)
    tq = _tile(l, 256)
    nt = (((1,), (1,)), ((), ()))
    for h in range(heads):
        c = (h * MLA_ROPE) // LANES

        def tile(t, carry, h=h, c=c):
            rows = slice(0, tq) if l == tq else pl.ds(pl.multiple_of(t * tq, tq), tq)
            qrh = qr_scr[rows, c * LANES:(c + 1) * LANES]
            lq = lax.broadcasted_iota(I32, qrh.shape, 1)
            own = (lq // MLA_ROPE) == (h % (LANES // MLA_ROPE))
            s = lax.dot_general(qn_scr[rows, h * MLA_NOPE:(h + 1) * MLA_NOPE].astype(BF16),
                                kn_scr[:, h * MLA_NOPE:(h + 1) * MLA_NOPE], nt, preferred_element_type=F32)
            s += lax.dot_general(jnp.where(own, qrh, 0.0).astype(BF16), kr_scr[...], nt, preferred_element_type=F32)
            e = jnp.exp(s - jnp.max(s, axis=-1, keepdims=True))
            o = jnp.dot(e.astype(BF16), vv_scr[:, h * MLA_V:(h + 1) * MLA_V], preferred_element_type=F32)
            o = o * (1.0 / jnp.sum(e, axis=-1, keepdims=True))
            o_ref[rows, h * MLA_V:(h + 1) * MLA_V] = o.astype(o_ref.dtype)
            return carry

        if l == tq:
            tile(0, 0)
        else:
            lax.fori_loop(0, l // tq, tile, 0)


def _mla(xin, col0, qn, kvn, w_uq, w_ukv, b, l, row_blk0, cache=None, layer=0):
    heads = w_uq.shape[1] // (MLA_NOPE + MLA_ROPE)
    wq = w_uq.reshape(MLA_Q_LORA, heads, MLA_NOPE + MLA_ROPE)
    wqn = wq[:, :, :MLA_NOPE].reshape(MLA_Q_LORA, heads * MLA_NOPE).astype(BF16)
    wqr = wq[:, :, MLA_NOPE:].reshape(MLA_Q_LORA, heads * MLA_ROPE).astype(BF16)
    wkv = w_ukv.reshape(MLA_KV_LORA, heads, MLA_NOPE + MLA_V)
    wuk = wkv[:, :, :MLA_NOPE].reshape(MLA_KV_LORA, heads * MLA_NOPE).astype(BF16)
    wuv = wkv[:, :, MLA_NOPE:].reshape(MLA_KV_LORA, heads * MLA_V).astype(BF16)
    has_cache = cache is not None
    c_kv = col0 + MLA_Q_LORA
    c_kr = c_kv + MLA_KV_LORA
    full = lambda a: pl.BlockSpec(a.shape, lambda i: (0,) * a.ndim)
    in_specs = [pl.BlockSpec((l, MLA_Q_LORA), lambda i: (row_blk0 + i, col0 // MLA_Q_LORA)),
                pl.BlockSpec((l, MLA_KV_LORA), lambda i: (row_blk0 + i, c_kv // MLA_KV_LORA)),
                pl.BlockSpec((l, LANES), lambda i: (row_blk0 + i, c_kr // LANES)),
                pl.BlockSpec((1, MLA_Q_LORA), lambda i: (0, 0)),
                pl.BlockSpec((1, MLA_KV_LORA), lambda i: (0, 0)),
                full(wqn), full(wqr), full(wuk), full(wuv)]
    args = [xin, xin, xin, qn.reshape(1, -1), kvn.reshape(1, -1), wqn, wqr, wuk, wuv]
    o_shape = jax.ShapeDtypeStruct((b * l, heads * MLA_V), BF16)
    o_spec = pl.BlockSpec((l, heads * MLA_V), lambda i: (i, 0))
    lk = l
    if has_cache:
        cc, ckr2 = cache
        p = cc.shape[2]
        lk = l + p
        cos, sin = _rope_tables(l, MLA_ROPE)
        in_specs += [pl.BlockSpec((None, None, p, MLA_KV_LORA), lambda i: (i, layer, 0, 0)),
                     pl.BlockSpec((None, None, p, LANES), lambda i: (i, layer, 0, 0)),
                     pl.BlockSpec((l, LANES), lambda i: (0, 0)), pl.BlockSpec((l, LANES), lambda i: (0, 0))]
        args += [cc, ckr2, cos, sin]
        out_shape, out_specs = o_shape, o_spec
    else:
        out_shape = (o_shape, jax.ShapeDtypeStruct((b * l, MLA_KV_LORA), F32))
        out_specs = (o_spec, pl.BlockSpec((l, MLA_KV_LORA), lambda i: (i, 0)))
    scratch = [pltpu.VMEM((l, heads * MLA_NOPE), F32), pltpu.VMEM((l, heads * MLA_ROPE), F32),
               pltpu.VMEM((lk, heads * MLA_NOPE), BF16), pltpu.VMEM((lk, heads * MLA_V), BF16),
               pltpu.VMEM((lk, LANES), BF16)]
    return pl.pallas_call(
        functools.partial(_mla_kernel, has_cache=has_cache, heads=heads), name="mla_lat" if has_cache else "mla_ctx",
        out_shape=out_shape, grid=(b,), in_specs=in_specs, out_specs=out_specs, scratch_shapes=scratch,
        compiler_params=_cparams(1),
    )(*args)


def _cmul(ar, ai, br, bi):
    return ar * br - ai * bi, ar * bi + ai * br


def _s5_operators(lam_re, lam_im, log_dt, b_re, b_im, c_re, c_im, d_skip):
    g, p = lam_re.shape[1:]
    t = SSM_T
    n = jnp.arange(t + 1, dtype=F32)[:, None, None]
    us, ws, ks, a_rows = [], [], [], []
    for d in range(2):
        lr, li = lam_re[d], lam_im[d]
        dt = jnp.exp(log_dt[d])[:, None]
        pw_re = jnp.exp(n * (lr * dt)[None]) * jnp.cos(n * (li * dt)[None])
        pw_im = jnp.exp(n * (lr * dt)[None]) * jnp.sin(n * (li * dt)[None])
        den = lr * lr + li * li
        cf_re, cf_im = _cmul(pw_re[1] - 1.0, pw_im[1], lr / den, -li / den)
        bb_re, bb_im = _cmul(cf_re[..., None], cf_im[..., None], b_re[d], b_im[d])
        cl_re, cl_im = _cmul(c_re[None], c_im[None], pw_re[:t, :, None, :], pw_im[:t, :, None, :])
        ks.append(jnp.einsum("ngcp,gpe->ngce", cl_re, bb_re, precision=HIGHEST)
                  - jnp.einsum("ngcp,gpe->ngce", cl_im, bb_im, precision=HIGHEST))
        e = jnp.arange(t)[::-1] if d == 0 else jnp.arange(t)
        u_re, u_im = _cmul(pw_re[e][:, :, None, :], pw_im[e][:, :, None, :],
                           jnp.swapaxes(bb_re, 1, 2)[None], jnp.swapaxes(bb_im, 1, 2)[None])
        us += [jnp.moveaxis(u_re, 1, 0).reshape(g, t * SSM_CH, p), jnp.moveaxis(u_im, 1, 0).reshape(g, t * SSM_CH, p)]
        e = jnp.arange(1, t + 1) if d == 0 else jnp.arange(t, 0, -1)
        w_re, w_im = _cmul(c_re[None], c_im[None], pw_re[e][:, :, None, :], pw_im[e][:, :, None, :])
        to_w = lambda a: jnp.transpose(a, (1, 3, 0, 2)).reshape(g, p, t * SSM_CH)
        ws += [to_w(w_re), -to_w(w_im)]
        a_rows += [jnp.concatenate([pw_re[t], pw_re[t]], -1), jnp.concatenate([-pw_im[t], pw_im[t]], -1)]
    s_in = jnp.arange(t)[:, None]
    s_out = jnp.arange(t)[None, :]
    kf = jnp.where((s_out >= s_in)[..., None, None, None], ks[0][jnp.clip(s_out - s_in, 0, t - 1)], 0.0)
    kb = jnp.where((s_in >= s_out)[..., None, None, None], ks[1][jnp.clip(s_in - s_out, 0, t - 1)], 0.0)
    skip = (s_in == s_out)[..., None, None, None] * (d_skip.reshape(g, SSM_CH)[None, None, :, :, None]
                                                     * jnp.eye(SSM_CH, dtype=F32)[None, None, None])
    tg = jnp.transpose(kf + kb + skip, (2, 0, 4, 1, 3)).reshape(g, t * SSM_CH, t * SSM_CH)
    u_all = jnp.concatenate(us, axis=-1)
    w_all = jnp.concatenate([tg] + ws, axis=1)
    a_all = jnp.stack(a_rows + a_rows, axis=1)
    return u_all, w_all, a_all


def _gelu_tanh(x):
    return 0.5 * x * (1.0 + jnp.tanh(0.7978845608028654 * (x + 0.044715 * (x * x * x))))


def _s5_kernel(x_ref, u_ref, w_ref, a_ref, h0_ref, y_ref, fin_ref, v_scr, hp_scr, gn_scr, *, nb):
    rows = x_ref.shape[0]
    nj = rows // nb
    x = x_ref[...]
    v_scr[...] = jnp.dot(x, u_ref[...], precision=HIGHEST, preferred_element_type=F32)
    two_p = a_ref.shape[1]
    a = a_ref[...]

    def fwd(j, h):
        r0 = pl.multiple_of(j * nb, nb)
        hp_scr[pl.ds(r0, nb), :] = h
        return a[0:1] * h + a[1:2] * pltpu.roll(h, two_p // 2, 1) + v_scr[pl.ds(r0, nb), 0:two_p]

    h_fin = lax.fori_loop(0, nj, fwd, h0_ref[0])

    def bwd(i, h):
        r0 = pl.multiple_of((nj - 1 - i) * nb, nb)
        gn_scr[pl.ds(r0, nb), :] = h
        return a[2:3] * h + a[3:4] * pltpu.roll(h, two_p // 2, 1) + v_scr[pl.ds(r0, nb), two_p:2 * two_p]

    g_fin = lax.fori_loop(0, nj, bwd, h0_ref[1])
    fin_ref[0] = h_fin
    fin_ref[1] = g_fin
    tc = x.shape[1]
    y = jnp.dot(x, w_ref[0:tc, :], precision=HIGHEST, preferred_element_type=F32)
    y += jnp.dot(hp_scr[...], w_ref[tc:tc + two_p, :], precision=HIGHEST, preferred_element_type=F32)
    y += jnp.dot(gn_scr[...], w_ref[tc + two_p:tc + 2 * two_p, :], precision=HIGHEST, preferred_element_type=F32)
    y_ref[...] = _gelu_tanh(y)


def _s5_scan(xg, u_all, w_all, a_all, h0, nb):
    g, rows, tc = xg.shape
    two_p = a_all.shape[2]
    blk = lambda a: pl.BlockSpec((None,) + a.shape[1:], lambda i: (i,) + (0,) * (a.ndim - 1))
    return pl.pallas_call(
        functools.partial(_s5_kernel, nb=nb), name="s5",
        out_shape=(jax.ShapeDtypeStruct((g, rows, tc), F32), jax.ShapeDtypeStruct((g, 2, nb, two_p), F32)),
        grid=(g,),
        in_specs=[blk(xg), blk(u_all), blk(w_all), blk(a_all), blk(h0)],
        out_specs=(pl.BlockSpec((None, rows, tc), lambda i: (i, 0, 0)),
                   pl.BlockSpec((None, 2, nb, two_p), lambda i: (i, 0, 0, 0))),
        scratch_shapes=[pltpu.VMEM((rows, 2 * two_p), F32), pltpu.VMEM((rows, two_p), F32),
                        pltpu.VMEM((rows, two_p), F32)],
        compiler_params=_cparams(1),
    )(xg, u_all, w_all, a_all, h0)


def _glu_kernel(y_ref, w_ref, b_ref, o_ref):
    y = y_ref[...]
    z = jnp.dot(y.astype(BF16), w_ref[...], preferred_element_type=F32) + b_ref[...]
    o_ref[...] = (y * _sigmoid(z)).astype(o_ref.dtype)


def _glu(y, w, b):
    m, n = y.shape
    tm = _tile(m, 512)
    return pl.pallas_call(
        _glu_kernel, name="s5_glu",
        out_shape=jax.ShapeDtypeStruct((m, n), BF16),
        grid=(m // tm,),
        in_specs=[pl.BlockSpec((tm, n), lambda i: (i, 0)), pl.BlockSpec((n, n), lambda i: (0, 0)),
                  pl.BlockSpec((1, n), lambda i: (0, 0))],
        out_specs=pl.BlockSpec((tm, n), lambda i: (i, 0)),
        compiler_params=_cparams(1),
    )(y, w.astype(BF16), b.reshape(1, n))


def _s5_path(u, ops, b, l, h0):
    g = ops[0].shape[0]
    nj = l // SSM_T
    xg = u.reshape(b, nj, SSM_T, g, SSM_CH).transpose(3, 1, 0, 2, 4).reshape(g, nj * b, SSM_T * SSM_CH)
    yg, fin = _s5_scan(xg, *ops, h0, b)
    y = yg.reshape(g, nj, b, SSM_T, SSM_CH).transpose(2, 1, 3, 0, 4).reshape(b * l, g * SSM_CH)
    return y, fin


def _softplus(x):
    return jnp.maximum(x, 0.0) + jnp.log1p(jnp.exp(-jnp.abs(x)))


def _dn_conv(x_ref, w_ref):
    x = x_ref[...]
    l = x.shape[0]
    w = w_ref[...]
    t = lax.broadcasted_iota(I32, x.shape, 0)
    pad = (DN_CONV - 1) // 2
    acc = x * w[pad:pad + 1]
    for tau in range(DN_CONV):
        delta = tau - pad
        if delta == 0:
            continue
        shifted = pltpu.roll(x, (-delta) % l, 0)
        ok = (t + delta >= 0) & (t + delta < l)
        acc += jnp.where(ok, shifted, 0.0) * w[tau:tau + 1]
    return _silu(acc)


def _l2(x):
    return x * lax.rsqrt(jnp.sum(x * x, axis=-1, keepdims=True) + EPS)


def _dn_chunk(d, r0, q_scr, k_scr, v_scr, gb_scr, s_scr, o_scr):
    c = DN_CHUNK
    hd = DN_HEAD_DIM
    rows = pl.ds(r0, c)
    q, k, v = q_scr[rows, :], k_scr[rows, :], v_scr[rows, :]
    g = gb_scr[rows, 2 * d * hd:(2 * d + 1) * hd]
    beta = gb_scr[rows, (2 * d + 1) * hd:(2 * d + 2) * hd]
    ri = lax.broadcasted_iota(I32, (c, c), 0)
    ci = lax.broadcasted_iota(I32, (c, c), 1)
    incl = (ci <= ri) if d == 0 else (ci >= ri)
    strict = (ci < ri) if d == 0 else (ci > ri)
    gc = jnp.dot(incl.astype(F32), g, precision=HIGHEST, preferred_element_type=F32)
    lane = lax.broadcasted_iota(I32, gc.shape, 1)
    gc_row = lax.dot_general(jnp.ones((c, hd), F32), jnp.where(lane == 0, gc, 0.0), (((1,), (1,)), ((), ())),
                             precision=HIGHEST, preferred_element_type=F32)
    dmat = jnp.exp(jnp.where(incl, gc[:, 0:c] - gc_row, NEG_BIG))
    kb = k * beta
    a = jnp.where(strict, _bdot_nt(kb, k) * dmat, 0.0)
    qk = _bdot_nt(q, k) * dmat
    inv = jnp.where(ri == ci, 1.0, 0.0) - a
    pw = a
    for _ in range(int(math.log2(c)) - 1):
        pw = _dot3(pw, pw)
        inv = inv + _dot3(inv, pw)
    egc = jnp.exp(gc)
    u = _dot3(inv, v * beta)
    w = _dot3(inv, kb * egc)
    tot = gc[c - 1:c, :] if d == 0 else gc[0:1, :]
    s = s_scr[d]
    v_new = u - _bdot(w, s)
    o_scr[d, rows, :] = _bdot(q * egc, s) + _bdot(qk, v_new)
    s_scr[d] = s * jnp.exp(tot) + _bdot_tn(k * jnp.exp(tot - gc), v_new)


def _dn_kernel(*refs, has_state, nh):
    if has_state:
        (q_ref, k_ref, v_ref, z_ref, ab_ref, wq_ref, wk_ref, wv_ref, alog_ref, bias_ref, ng_ref, s0_ref, o_ref,
         q_scr, k_scr, v_scr, gb_scr, s_scr, o_scr) = refs
    else:
        (q_ref, k_ref, v_ref, z_ref, ab_ref, wq_ref, wk_ref, wv_ref, alog_ref, bias_ref, ng_ref, o_ref, sfin_ref,
         q_scr, k_scr, v_scr, gb_scr, s_scr, o_scr) = refs
    hd = DN_HEAD_DIM
    h = pl.program_id(1)
    l = q_ref.shape[0]
    n = l // DN_CHUNK
    q_scr[...] = _l2(_dn_conv(q_ref, wq_ref)) * hd ** -0.5
    k_scr[...] = _l2(_dn_conv(k_ref, wk_ref))
    v_scr[...] = _dn_conv(v_ref, wv_ref)
    x = ab_ref[...]
    lane = lax.broadcasted_iota(I32, x.shape, 1)
    is_decay = ((lane // nh) % 2) == 0
    mix = jnp.where(is_decay, -jnp.exp(alog_ref[...]) * _softplus(x + bias_ref[...]), _sigmoid(x))
    sr = lax.broadcasted_iota(I32, (LANES, 4 * hd), 0)
    sc = lax.broadcasted_iota(I32, (LANES, 4 * hd), 1)
    sel = (sr == LANES // 2 + nh * (sc // hd) + h).astype(F32)
    gb_scr[...] = jnp.dot(mix, sel, precision=HIGHEST, preferred_element_type=F32)
    if has_state:
        s_scr[...] = s0_ref[...]
    else:
        s_scr[...] = jnp.zeros(s_scr.shape, F32)

    def body(c, carry):
        _dn_chunk(0, pl.multiple_of(c * DN_CHUNK, DN_CHUNK), q_scr, k_scr, v_scr, gb_scr, s_scr, o_scr)
        _dn_chunk(1, pl.multiple_of((n - 1 - c) * DN_CHUNK, DN_CHUNK), q_scr, k_scr, v_scr, gb_scr, s_scr, o_scr)
        return carry

    lax.fori_loop(0, n, body, 0)
    o = _rms(o_scr[0] + o_scr[1], ng_ref[...]) * _silu(z_ref[...])
    o_ref[...] = o.astype(o_ref.dtype)
    if not has_state:
        sfin_ref[...] = s_scr[...]


def _deltanet(xin, ab_col, conv_w, a_log, dt_bias, norm_g, b, l, row_blk0, state=None, layer=0):
    hd = DN_HEAD_DIM
    nh = a_log.shape[1]
    has_state = state is not None
    lane_par = lambda p: jnp.zeros((1, LANES), F32).at[0, LANES // 2:LANES // 2 + 4 * nh].set(
        jnp.concatenate([p[0], jnp.zeros_like(p[0]), p[1], jnp.zeros_like(p[1])]))
    xspec = lambda sec: pl.BlockSpec((l, hd), lambda i, h: (row_blk0 + i, sec * nh + h))
    wspec = lambda sec: pl.BlockSpec((DN_CONV, hd), lambda i, h: (0, sec * nh + h))
    vec = pl.BlockSpec((1, LANES), lambda i, h: (0, 0))
    in_specs = [xspec(0), xspec(1), xspec(2), xspec(3),
                pl.BlockSpec((l, LANES), lambda i, h: (row_blk0 + i, ab_col // LANES)),
                wspec(0), wspec(1), wspec(2), vec, vec, vec]
    args = [xin, xin, xin, xin, xin, conv_w, conv_w, conv_w, lane_par(a_log), lane_par(dt_bias),
            norm_g.reshape(1, hd)]
    o_shape = jax.ShapeDtypeStruct((b * l, nh * hd), BF16)
    o_spec = pl.BlockSpec((l, hd), lambda i, h: (i, h))
    if has_state:
        in_specs.append(pl.BlockSpec((None, None, 2, None, hd, hd), lambda i, h: (i, layer, 0, h, 0, 0)))
        args.append(state)
        out_shape, out_specs = o_shape, o_spec
    else:
        out_shape = (o_shape, jax.ShapeDtypeStruct((b, 2, nh, hd, hd), F32))
        out_specs = (o_spec, pl.BlockSpec((None, 2, None, hd, hd), lambda i, h: (i, 0, h, 0, 0)))
    scratch = [pltpu.VMEM((l, hd), F32), pltpu.VMEM((l, hd), F32), pltpu.VMEM((l, hd), F32),
               pltpu.VMEM((l, 4 * hd), F32), pltpu.VMEM((2, hd, hd), F32), pltpu.VMEM((2, l, hd), F32)]
    return pl.pallas_call(
        functools.partial(_dn_kernel, has_state=has_state, nh=nh), name="deltanet_lat" if has_state else "deltanet_ctx",
        out_shape=out_shape, grid=(b, nh), in_specs=in_specs, out_specs=out_specs, scratch_shapes=scratch,
        compiler_params=_cparams(2),
    )(*args)


def _round_up(n, k):
    return -(-n // k) * k


def kernel(x_prompt, x_sample, cache_gqa_k, cache_gqa_v, cache_mla_ckv, cache_mla_krope, state_dn, state_ssm_re, state_ssm_im, c, c_ctx, w_ada, b_ada, norm1_g, norm2_g, w_in, dn_conv, dn_a_log, dn_dt_bias, dn_norm_g, ssm_lam_re, ssm_lam_im, ssm_log_dt, ssm_b_re, ssm_b_im, ssm_c_re, ssm_c_im, ssm_d, ssm_w_glu, ssm_b_glu, gqa_q_norm, gqa_k_norm, mla_q_norm, mla_kv_norm, mla_w_uq, mla_w_ukv, w_out, moe_w_router, moe_b_router, moe_w_gate, moe_b_gate, moe_w_up, moe_b_up, moe_w_down, moe_b_down, norm_f):
    bc, lc, d = x_prompt.shape
    bl, ll, _ = x_sample.shape
    depth = w_in.shape[0]
    m_ctx = bc * lc
    assert m_ctx % ll == 0 and lc % DN_CHUNK == 0 and ll % DN_CHUNK == 0
    lat_blk0 = m_ctx // ll
    gw = d // 4
    nh = dn_a_log.shape[2]
    n_gqa = gw + 2 * GQA_KV * GQA_HD
    n_mla = MLA_Q_LORA + MLA_KV_LORA + MLA_ROPE
    dn_in = 4 * gw + 4 * nh
    assert w_in.shape[2] == dn_in + gw + n_gqa + n_mla

    c_ssm = 4 * gw
    c_gqa = c_ssm + gw
    q0, k0, v0 = c_gqa, c_gqa + gw, c_gqa + gw + GQA_KV * GQA_HD
    c_mla = _round_up(c_gqa + n_gqa, MLA_Q_LORA)
    c_kr = c_mla + MLA_Q_LORA + MLA_KV_LORA
    assert (c_mla + MLA_Q_LORA) % MLA_KV_LORA == 0 and c_kr % LANES == 0 and MLA_ROPE + 4 * nh <= LANES
    n_w = _round_up(c_kr + LANES, 512)
    o_ssm, o_gqa, o_mla = dn_in, dn_in + gw, dn_in + gw + n_gqa
    zpad = lambda n: jnp.zeros((depth, d, n), F32)
    w_in_p = jnp.concatenate(
        [w_in[:, :, :4 * gw], w_in[:, :, o_ssm:o_gqa], w_in[:, :, o_gqa:o_mla], zpad(c_mla - c_gqa - n_gqa),
         w_in[:, :, o_mla:o_mla + MLA_Q_LORA + MLA_KV_LORA],
         w_in[:, :, o_mla + MLA_Q_LORA + MLA_KV_LORA:], zpad(LANES // 2 - MLA_ROPE), w_in[:, :, 4 * gw:dn_in],
         zpad(n_w - c_kr - LANES // 2 - 4 * nh)], axis=-1).astype(BF16)
    w_out_b = w_out.astype(BF16)

    x = jnp.concatenate([x_prompt.reshape(m_ctx, d), x_sample.reshape(bl * ll, d)], axis=0)
    n_cond = _round_up(1 + bl, 8)
    cond = jnp.zeros((n_cond, d), F32).at[0].set(c_ctx).at[1:1 + bl].set(c)

    p_len = cache_gqa_k.shape[2]
    ck = cache_gqa_k.reshape(bl, depth, p_len, GQA_KV * GQA_HD)
    cv = cache_gqa_v.reshape(bl, depth, p_len, GQA_KV * GQA_HD)
    ckr2 = jnp.concatenate([cache_mla_krope, cache_mla_krope], axis=-1)
    n_grp, n_p = ssm_lam_re.shape[2:]

    outs = [[] for _ in range(7)]
    for l in range(depth):
        mod = _ada(cond, w_ada[l], b_ada[l]).reshape(n_cond, 6, 1, d)
        xin = _inproj(x, mod, norm1_g[l], w_in_p[l], m_ctx, ll)

        dn_args = (xin, c_kr, dn_conv[l], dn_a_log[l], dn_dt_bias[l], dn_norm_g[l])
        o_dn_c, dn_fin = _deltanet(*dn_args, bc, lc, 0)
        o_dn_l = _deltanet(*dn_args, bl, ll, lat_blk0, state=state_dn, layer=l)

        ops = _s5_operators(ssm_lam_re[l], ssm_lam_im[l], ssm_log_dt[l], ssm_b_re[l], ssm_b_im[l],
                            ssm_c_re[l], ssm_c_im[l], ssm_d[l])
        u = xin[:, c_ssm:c_ssm + gw]
        y_c, ssm_fin = _s5_path(u[:m_ctx], ops, bc, lc, jnp.zeros((n_grp, 2, bc, 2 * n_p), F32))
        h0 = jnp.concatenate([state_ssm_re[:, l], state_ssm_im[:, l]], axis=-1).transpose(2, 1, 0, 3)
        y_l, _ = _s5_path(u[m_ctx:], ops, bl, ll, h0)
        o_s5 = _glu(jnp.concatenate([y_c, y_l], axis=0), ssm_w_glu[l], ssm_b_glu[l])

        gqa_args = (xin, (q0, k0, v0), gqa_q_norm[l], gqa_k_norm[l])
        o_gqa_c, k_new = _gqa(*gqa_args, bc, lc, 0)
        o_gqa_l = _gqa(*gqa_args, bl, ll, lat_blk0, cache=(ck, cv), layer=l)

        mla_args = (xin, c_mla, mla_q_norm[l], mla_kv_norm[l], mla_w_uq[l], mla_w_ukv[l])
        o_mla_c, ckv_new = _mla(*mla_args, bc, lc, 0)
        o_mla_l = _mla(*mla_args, bl, ll, lat_blk0, cache=(cache_mla_ckv, ckr2), layer=l)

        cat = lambda a, b_: jnp.concatenate([a, b_], axis=0)
        parts = [cat(o_dn_c, o_dn_l), o_s5, cat(o_gqa_c, o_gqa_l), cat(o_mla_c, o_mla_l)]
        x = _outproj(parts, w_out_b[l], x, mod, m_ctx, ll)
        x = _moe(x, mod, norm2_g[l], moe_w_router[l], moe_b_router[l], moe_w_gate[l], moe_b_gate[l],
                 moe_w_up[l], moe_b_up[l], moe_w_down[l], moe_b_down[l], norm_f, m_ctx, ll,
                 final_norm=(l == depth - 1))

        fin = ssm_fin.transpose(2, 1, 0, 3)
        outs[0].append(k_new.reshape(bc, lc, GQA_KV, GQA_HD))
        outs[1].append(xin[:m_ctx, v0:v0 + GQA_KV * GQA_HD].reshape(bc, lc, GQA_KV, GQA_HD))
        outs[2].append(ckv_new.reshape(bc, lc, MLA_KV_LORA))
        outs[3].append(xin[:m_ctx, c_kr:c_kr + MLA_ROPE].reshape(bc, lc, MLA_ROPE))
        outs[4].append(dn_fin)
        outs[5].append(fin[..., :n_p])
        outs[6].append(fin[..., n_p:])

    y_prompt = x[:m_ctx].reshape(bc, lc, d)
    y_sample = x[m_ctx:].reshape(bl, ll, d)
    return (y_prompt, y_sample) + tuple(jnp.stack(o, axis=1) for o in outs)
```

```python
import functools
import math

import jax
import jax.numpy as jnp
from jax import lax
from jax.experimental import pallas as pl
from jax.experimental.pallas import tpu as pltpu

F32 = jnp.float32
BF16 = jnp.bfloat16
I32 = jnp.int32

EPS = 1e-6
ROPE_THETA = 10000.0
GRID_W = 64
N_EXPERTS = 32
TOP_K = 4
SWIGLU_LIMIT = 7.0
SWIGLU_ALPHA = 1.702
DN_CONV = 5
DN_CHUNK = 64
DN_HEAD_DIM = 128
SSM_CH = 16
SSM_P = 64
SSM_T = 16
GQA_HD = 128
GQA_KV = 2
MLA_NOPE = 128
MLA_ROPE = 64
MLA_V = 128
MLA_Q_LORA = 768
MLA_KV_LORA = 256

LANES = 128
VMEM_LIMIT = 56 * 1024 * 1024
NEG_BIG = -1e30
HIGHEST = lax.Precision.HIGHEST


def _cparams(n_axes):
    return pltpu.CompilerParams(dimension_semantics=("arbitrary",) * n_axes, vmem_limit_bytes=VMEM_LIMIT)


def _tile(n, want):
    t = min(n, want)
    while n % t:
        t //= 2
    return t


def _bdot(a, b):
    return jnp.dot(a.astype(BF16), b.astype(BF16), preferred_element_type=F32)


def _bdot_nt(a, b):
    return lax.dot_general(a.astype(BF16), b.astype(BF16), (((1,), (1,)), ((), ())), preferred_element_type=F32)


def _bdot_tn(a, b):
    return lax.dot_general(a.astype(BF16), b.astype(BF16), (((0,), (0,)), ((), ())), preferred_element_type=F32)


def _split(a):
    hi = a.astype(BF16)
    lo = (a - hi.astype(F32)).astype(BF16)
    return hi, lo


def _dot3(a, b):
    ah, al = _split(a)
    bh, bl = _split(b)
    d = functools.partial(jnp.dot, preferred_element_type=F32)
    return d(ah, bh) + (d(ah, bl) + d(al, bh))


def _sigmoid(x):
    return 1.0 / (1.0 + jnp.exp(-x))


def _silu(x):
    return x * _sigmoid(x)


def _rms(x, g):
    return x * lax.rsqrt(jnp.mean(x * x, axis=-1, keepdims=True) + EPS) * g


def _ada_kernel(c_ref, w_ref, b_ref, o_ref):
    o_ref[...] = _bdot(_silu(c_ref[...]), w_ref[...]) + b_ref[...]


def _ada(cond, w, b, layer):
    r, d = cond.shape
    n = w.shape[2]
    tn = _tile(n, 512)
    return pl.pallas_call(
        _ada_kernel, name="ada",
        out_shape=jax.ShapeDtypeStruct((r, n), F32),
        grid=(n // tn,),
        in_specs=[pl.BlockSpec((r, d), lambda j: (0, 0)),
                  pl.BlockSpec((None, d, tn), lambda j: (layer, 0, j)),
                  pl.BlockSpec((1, tn), lambda j: (0, j))],
        out_specs=pl.BlockSpec((r, tn), lambda j: (0, j)),
        compiler_params=_cparams(1),
    )(cond, w, b.reshape(1, n))


def _mod_row(i, tm, m_ctx, l_lat):
    r0 = i * tm
    return jnp.where(r0 < m_ctx, 0, 1 + (r0 - m_ctx) // l_lat)


def _mod_spec(d, which, tm, m_ctx, l_lat, n_grid):
    if n_grid == 1:
        return pl.BlockSpec((None, None, 1, d), lambda i: (_mod_row(i, tm, m_ctx, l_lat), which, 0, 0))
    return pl.BlockSpec((None, None, 1, d), lambda i, j: (_mod_row(i, tm, m_ctx, l_lat), which, 0, 0))


def _inproj_kernel(x_ref, sh_ref, sc_ref, g_ref, w_ref, o_ref, h_ref):
    @pl.when(pl.program_id(1) == 0)
    def _():
        y = _rms(x_ref[...], g_ref[...])
        h_ref[...] = (y * (1.0 + sc_ref[...]) + sh_ref[...]).astype(BF16)

    o_ref[...] = jnp.dot(h_ref[...], w_ref[...], preferred_element_type=F32)


def _inproj(x, mod4, g, w, layer, m_ctx, l_lat):
    m, d = x.shape
    n = w.shape[2]
    tm = _tile(math.gcd(m_ctx, l_lat), 512)
    tn = _tile(n, 512)
    if n % tn or tn % LANES:
        tn = LANES
    return pl.pallas_call(
        _inproj_kernel, name="inproj",
        out_shape=jax.ShapeDtypeStruct((m, n), F32),
        grid=(m // tm, n // tn),
        in_specs=[pl.BlockSpec((tm, d), lambda i, j: (i, 0)),
                  _mod_spec(d, 0, tm, m_ctx, l_lat, 2),
                  _mod_spec(d, 1, tm, m_ctx, l_lat, 2),
                  pl.BlockSpec((1, d), lambda i, j: (0, 0)),
                  pl.BlockSpec((None, d, tn), lambda i, j: (layer, 0, j))],
        out_specs=pl.BlockSpec((tm, tn), lambda i, j: (i, j)),
        scratch_shapes=[pltpu.VMEM((tm, d), BF16)],
        compiler_params=_cparams(2),
    )(x, mod4, mod4, g.reshape(1, d), w)


def _outproj_kernel(a0, a1, a2, a3, w_ref, x_ref, g_ref, o_ref):
    gw = a0.shape[1]
    acc = jnp.dot(a0[...], w_ref[0:gw, :], preferred_element_type=F32)
    acc += jnp.dot(a1[...], w_ref[gw:2 * gw, :], preferred_element_type=F32)
    acc += jnp.dot(a2[...], w_ref[2 * gw:3 * gw, :], preferred_element_type=F32)
    acc += jnp.dot(a3[...], w_ref[3 * gw:4 * gw, :], preferred_element_type=F32)
    o_ref[...] = x_ref[...] + g_ref[...] * acc


def _outproj(parts, w, layer, x, mod4, m_ctx, l_lat):
    m, d = x.shape
    gw = parts[0].shape[1]
    tm = _tile(math.gcd(m_ctx, l_lat), 512)
    tn = _tile(d, 512)
    part_spec = pl.BlockSpec((tm, gw), lambda i, j: (i, 0))
    g_spec = pl.BlockSpec((None, None, 1, tn), lambda i, j: (_mod_row(i, tm, m_ctx, l_lat), 2, 0, j))
    return pl.pallas_call(
        _outproj_kernel, name="outproj",
        out_shape=jax.ShapeDtypeStruct((m, d), F32),
        grid=(m // tm, d // tn),
        in_specs=[part_spec] * 4 + [pl.BlockSpec((None, 4 * gw, tn), lambda i, j: (layer, 0, j)),
                                    pl.BlockSpec((tm, tn), lambda i, j: (i, j)),
                                    g_spec],
        out_specs=pl.BlockSpec((tm, tn), lambda i, j: (i, j)),
        compiler_params=_cparams(2),
    )(*parts, w, x, mod4)


def _router_kernel(x_ref, sh_ref, sc_ref, g_ref, wr_ref, br_ref, h_ref, idx_ref, gate_ref):
    y = _rms(x_ref[...], g_ref[...])
    h = y * (1.0 + sc_ref[...]) + sh_ref[...]
    half = h.shape[1] // 2
    bits = pltpu.bitcast(h.astype(BF16).astype(F32), jnp.uint32)
    h_ref[...] = bits[:, half:] | lax.shift_right_logical(bits[:, :half], jnp.uint32(16))
    logits =jnp.dot(h, wr_ref[...], precision=HIGHEST, preferred_element_type=F32) + br_ref[...]
    lane = lax.broadcasted_iota(I32, logits.shape, 1).astype(F32)
    cur = logits
    vals, idxs = [], []
    for _ in range(TOP_K):
        mx = jnp.max(cur, axis=-1, keepdims=True)
        am = jnp.min(jnp.where(cur == mx, lane, float(LANES)), axis=-1, keepdims=True)
        vals.append(mx)
        idxs.append(am)
        cur = jnp.where(lane == am, -jnp.inf, cur)
    es = [jnp.exp(v - vals[0]) for v in vals]
    tot = es[0] + es[1] + es[2] + es[3]
    idx_o = jnp.zeros(logits.shape, F32)
    gate_o = jnp.zeros(logits.shape, F32)
    for k in range(TOP_K):
        idx_o = jnp.where(lane == float(k), idxs[k], idx_o)
        gate_o = jnp.where(lane == float(k), es[k] / tot, gate_o)
    idx_ref[...] = idx_o.astype(I32)
    gate_ref[...] = gate_o


def _router(x, mod4, g, wr, br, m_ctx, l_lat):
    m, d = x.shape
    tm = _tile(math.gcd(m_ctx, l_lat), 256)
    wr_p = jnp.zeros((d, LANES), F32).at[:, :N_EXPERTS].set(wr)
    br_p = jnp.full((1, LANES), NEG_BIG, F32).at[0, :N_EXPERTS].set(br)
    return pl.pallas_call(
        _router_kernel, name="router",
        out_shape=(jax.ShapeDtypeStruct((m, d // 2), jnp.uint32),
                   jax.ShapeDtypeStruct((m, LANES), I32),
                   jax.ShapeDtypeStruct((m, LANES), F32)),
        grid=(m // tm,),
        in_specs=[pl.BlockSpec((tm, d), lambda i: (i, 0)),
                  _mod_spec(d, 3, tm, m_ctx, l_lat, 1),
                  _mod_spec(d, 4, tm, m_ctx, l_lat, 1),
                  pl.BlockSpec((1, d), lambda i: (0, 0)),
                  pl.BlockSpec((d, LANES), lambda i: (0, 0)),
                  pl.BlockSpec((1, LANES), lambda i: (0, 0))],
        out_specs=(pl.BlockSpec((tm, d // 2), lambda i: (i, 0)),
                   pl.BlockSpec((tm, LANES), lambda i: (i, 0)),
                   pl.BlockSpec((tm, LANES), lambda i: (i, 0))),
        compiler_params=_cparams(1),
    )(x, mod4, mod4, g.reshape(1, d), wr_p, br_p)


def _route_plan(top_idx, tm, n_tiles):
    m = top_idx.shape[0]
    flat_e = top_idx.reshape(-1)
    onehot = (flat_e[:, None] == jnp.arange(N_EXPERTS, dtype=I32)[None, :]).astype(I32)
    csum = jnp.cumsum(onehot, axis=0)
    rank = jnp.take_along_axis(csum, flat_e[:, None], axis=1)[:, 0] - 1
    counts = csum[-1]
    tiles_e = (counts + tm - 1) // tm
    tile_end = jnp.cumsum(tiles_e)
    tile_start = tile_end - tiles_e
    dest = tile_start[flat_e] * tm + rank
    row_tok = jnp.zeros((n_tiles * tm,), I32).at[dest].set(jnp.arange(m * TOP_K, dtype=I32) // TOP_K)
    return dest.astype(I32), row_tok, tiles_e.astype(I32), tile_start.astype(I32), tile_end.astype(I32)


def _schedule(tiles_e, tile_start, tile_end, n_tiles, nj):
    n_used = tile_end[-1]
    step = jnp.arange(n_tiles * nj, dtype=I32)
    s = jnp.minimum(step, n_used * nj - 1)
    e = jnp.minimum(jnp.sum((s[:, None] >= (tile_end * nj)[None, :]).astype(I32), axis=1), N_EXPERTS - 1)
    local = s - tile_start[e] * nj
    t_e = jnp.maximum(tiles_e[e], 1)
    j = (local // t_e).astype(I32)
    r = local - j * t_e
    tile = (tile_start[e] + r).astype(I32)
    valid = step < n_used * nj
    spare = step - n_used * nj
    out_tile = jnp.where(valid, tile, n_used + spare // nj).astype(I32)
    out_j = jnp.where(valid, j, spare % nj).astype(I32)
    flags = jnp.where(valid, 1 + 2 * (r == 0).astype(I32), 4).astype(I32)
    return e, j, tile, out_tile, out_j, flags


def _gather_kernel(tok_ref, nused_ref, h_hbm, o_ref, buf, sem):
    i = pl.program_id(0)
    tg = buf.shape[0]

    @pl.when(i < nused_ref[0])
    def _():
        def issue(r, c):
            t = tok_ref[i * tg + r]
            pltpu.make_async_copy(h_hbm.at[pl.ds(t, 1)], buf.at[pl.ds(r, 1)], sem).start()
            return c

        lax.fori_loop(0, tg, issue, 0)

        def wait(r, c):
            pltpu.make_async_copy(h_hbm.at[pl.ds(0, 1)], buf.at[pl.ds(r, 1)], sem).wait()
            return c

        lax.fori_loop(0, tg, wait, 0)
        w = buf[...]
        half = w.shape[1]
        o_ref[:, 0:half] = pltpu.bitcast(lax.shift_left(w, jnp.uint32(16)), F32).astype(BF16)
        o_ref[:, half:2 * half] = pltpu.bitcast(w & jnp.uint32(0xFFFF0000), F32).astype(BF16)

    @pl.when(i >= nused_ref[0])
    def _():
        o_ref[...] = jnp.zeros(o_ref.shape, o_ref.dtype)


def _dispatch(h, row_tok, n_used_rows_tiles, tg):
    d = 2 * h.shape[1]
    n_rows = row_tok.shape[0]
    grid_spec = pltpu.PrefetchScalarGridSpec(
        num_scalar_prefetch=2, grid=(n_rows // tg,),
        in_specs=[pl.BlockSpec(memory_space=pl.ANY)],
        out_specs=pl.BlockSpec((tg, d), lambda i, tok, nu: (i, 0)),
        scratch_shapes=[pltpu.VMEM((tg, d // 2), jnp.uint32), pltpu.SemaphoreType.DMA(())])
    return pl.pallas_call(
        _gather_kernel, name="dispatch", grid_spec=grid_spec,
        out_shape=jax.ShapeDtypeStruct((n_rows, d), BF16),
        compiler_params=_cparams(1),
    )(row_tok, n_used_rows_tiles, h)


def _moe_up_kernel(se, sj, st, so, soj, sf, x_ref, wg_ref, wu_ref, bg_ref, bu_ref, o_ref, wgb, wub):
    flag = sf[pl.program_id(0)]

    @pl.when((flag & 4) != 0)
    def _():
        o_ref[...] = jnp.zeros(o_ref.shape, o_ref.dtype)

    @pl.when((flag & 2) != 0)
    def _():
        wgb[...] = wg_ref[...].astype(BF16)
        wub[...] = wu_ref[...].astype(BF16)

    @pl.when((flag & 1) != 0)
    def _():
        x = x_ref[...]
        gate = jnp.dot(x, wgb[...], preferred_element_type=F32) + bg_ref[...]
        up = jnp.dot(x, wub[...], preferred_element_type=F32) + bu_ref[...]
        gate = jnp.minimum(gate, SWIGLU_LIMIT)
        up = jnp.clip(up, -SWIGLU_LIMIT, SWIGLU_LIMIT)
        o_ref[...] = ((up + 1.0) * gate * _sigmoid(SWIGLU_ALPHA * gate)).astype(BF16)


def _moe_up(xs, wg, wu, bg, bu, layer, sched, tm, tn):
    n_rows, d = xs.shape
    depth, e, _, f = wg.shape
    n_steps = sched[0].shape[0]
    w_spec = pl.BlockSpec((None, None, d, tn), lambda s, se, sj, st, so, soj, sf: (layer, se[s], 0, sj[s]))
    b_spec = pl.BlockSpec((None, None, 1, tn), lambda s, se, sj, st, so, soj, sf: (layer, se[s], 0, sj[s]))
    grid_spec = pltpu.PrefetchScalarGridSpec(
        num_scalar_prefetch=6, grid=(n_steps,),
        in_specs=[pl.BlockSpec((tm, d), lambda s, se, sj, st, so, soj, sf: (st[s], 0)),
                  w_spec, w_spec, b_spec, b_spec],
        out_specs=pl.BlockSpec((tm, tn), lambda s, se, sj, st, so, soj, sf: (so[s], soj[s])),
        scratch_shapes=[pltpu.VMEM((d, tn), BF16), pltpu.VMEM((d, tn), BF16)])
    return pl.pallas_call(
        _moe_up_kernel, name="moe_up", grid_spec=grid_spec,
        out_shape=jax.ShapeDtypeStruct((n_rows, f), BF16),
        compiler_params=_cparams(1),
    )(*sched, xs, wg, wu, bg.reshape(depth, e, 1, f), bu.reshape(depth, e, 1, f))


def _moe_down_kernel(se, sj, st, so, soj, sf, h_ref, wd_ref, bd_ref, o_ref, wdb):
    flag = sf[pl.program_id(0)]

    @pl.when((flag & 4) != 0)
    def _():
        o_ref[...] = jnp.zeros(o_ref.shape, o_ref.dtype)

    @pl.when((flag & 2) != 0)
    def _():
        wdb[...] = wd_ref[...].astype(BF16)

    @pl.when((flag & 1) != 0)
    def _():
        o_ref[...] = jnp.dot(h_ref[...], wdb[...], preferred_element_type=F32) + bd_ref[...]


def _moe_down(hid, wd, bd, layer, sched, tm, tn):
    n_rows, f = hid.shape
    depth, e, _, d = wd.shape
    n_steps = sched[0].shape[0]
    grid_spec = pltpu.PrefetchScalarGridSpec(
        num_scalar_prefetch=6, grid=(n_steps,),
        in_specs=[pl.BlockSpec((tm, f), lambda s, se, sj, st, so, soj, sf: (st[s], 0)),
                  pl.BlockSpec((None, None, f, tn), lambda s, se, sj, st, so, soj, sf: (layer, se[s], 0, sj[s])),
                  pl.BlockSpec((None, None, 1, tn), lambda s, se, sj, st, so, soj, sf: (layer, se[s], 0, sj[s]))],
        out_specs=pl.BlockSpec((tm, tn), lambda s, se, sj, st, so, soj, sf: (so[s], soj[s])),
        scratch_shapes=[pltpu.VMEM((f, tn), BF16)])
    return pl.pallas_call(
        _moe_down_kernel, name="moe_down", grid_spec=grid_spec,
        out_shape=jax.ShapeDtypeStruct((n_rows, d), F32),
        compiler_params=_cparams(1),
    )(*sched, hid, wd, bd.reshape(depth, e, 1, d))


def _combine_kernel(pos_ref, y_hbm, x_ref, gate_ref, g2_ref, nf_ref, o_ref, buf, sem, *, final_norm):
    i = pl.program_id(0)
    tc = x_ref.shape[0]

    def issue(t, c):
        for k in range(TOP_K):
            p = pos_ref[(i * tc + t) * TOP_K + k]
            pltpu.make_async_copy(y_hbm.at[pl.ds(p, 1)], buf.at[k, pl.ds(t, 1)], sem).start()
        return c

    lax.fori_loop(0, tc, issue, 0)

    def wait(t, c):
        for k in range(TOP_K):
            pltpu.make_async_copy(y_hbm.at[pl.ds(0, 1)], buf.at[k, pl.ds(t, 1)], sem).wait()
        return c

    lax.fori_loop(0, tc, wait, 0)
    gates = gate_ref[...]
    acc = gates[:, 0:1] * buf[0]
    for k in range(1, TOP_K):
        acc += gates[:, k:k + 1] * buf[k]
    out = x_ref[...] + g2_ref[...] * acc
    if final_norm:
        out = _rms(out, nf_ref[...])
    o_ref[...] = out


def _combine(y, pos, x, gates, mod4, norm_f, m_ctx, l_lat, final_norm):
    m, d = x.shape
    tc = _tile(math.gcd(m_ctx, l_lat), 128)
    grid_spec = pltpu.PrefetchScalarGridSpec(
        num_scalar_prefetch=1, grid=(m // tc,),
        in_specs=[pl.BlockSpec(memory_space=pl.ANY),
                  pl.BlockSpec((tc, d), lambda i, pos: (i, 0)),
                  pl.BlockSpec((tc, LANES), lambda i, pos: (i, 0)),
                  pl.BlockSpec((None, None, 1, d), lambda i, pos: (_mod_row(i, tc, m_ctx, l_lat), 5, 0, 0)),
                  pl.BlockSpec((1, d), lambda i, pos: (0, 0))],
        out_specs=pl.BlockSpec((tc, d), lambda i, pos: (i, 0)),
        scratch_shapes=[pltpu.VMEM((TOP_K, tc, d), F32), pltpu.SemaphoreType.DMA(())])
    return pl.pallas_call(
        functools.partial(_combine_kernel, final_norm=final_norm), name="combine", grid_spec=grid_spec,
        out_shape=jax.ShapeDtypeStruct((m, d), F32),
        compiler_params=_cparams(1),
    )(pos, y, x, gates, mod4, norm_f.reshape(1, d))


def _moe(x, mod4, norm2_g, wr, br, wg, bg, wu, bu, wd, bd, layer, norm_f, m_ctx, l_lat, final_norm):
    m, d = x.shape
    f = wg.shape[3]
    tm = _tile(m, 512)
    n_tiles = -(-(m * TOP_K + N_EXPERTS * (tm - 1)) // tm)
    h, top_idx, gates = _router(x, mod4, norm2_g, wr, br, m_ctx, l_lat)
    dest, row_tok, tiles_e, tile_start, tile_end = _route_plan(top_idx[:, :TOP_K], tm, n_tiles)
    xs = _dispatch(h, row_tok, tile_end[-1:], tm)
    tn_up = _tile(f, 512)
    sched_up = _schedule(tiles_e, tile_start, tile_end, n_tiles, f // tn_up)
    hid = _moe_up(xs, wg, wu, bg, bu, layer, sched_up, tm, tn_up)
    tn_dn = _tile(d, 1024)
    sched_dn = _schedule(tiles_e, tile_start, tile_end, n_tiles, d // tn_dn)
    y = _moe_down(hid, wd, bd, layer, sched_dn, tm, tn_dn)
    return _combine(y, dest, x, gates, mod4, norm_f, m_ctx, l_lat, final_norm)


def _rope_tables(l, d):
    t = jnp.arange(l, dtype=I32)
    row = (t // GRID_W).astype(F32)[:, None]
    col = (t % GRID_W).astype(F32)[:, None]
    q = d // 4
    lane = jnp.arange(LANES, dtype=I32) % d
    pos = jnp.where((lane // (2 * q))[None, :] == 0, row, col)
    inv = ROPE_THETA ** (-(lane % q).astype(F32) / q)
    ang = pos * inv[None, :]
    sign = jnp.where((lane % (2 * q)) < q, -1.0, 1.0)[None, :]
    return jnp.cos(ang), jnp.sin(ang) * sign


def _rope(x, cos, sin, d):
    q = d // 4
    lane = lax.broadcasted_iota(I32, x.shape, 1)
    fwd = pltpu.roll(x, LANES - q, 1)
    bwd = pltpu.roll(x, q, 1)
    return x * cos + jnp.where((lane % (2 * q)) < q, fwd, bwd) * sin


def _attend(q_scr, k_scr, v_scr, o_ref, col0, tq):
    dv = v_scr.shape[1]

    def body(i, c):
        r0 = pl.multiple_of(i * tq, tq)
        s = lax.dot_general(q_scr[pl.ds(r0, tq), :], k_scr[...], (((1,), (1,)), ((), ())),
                            preferred_element_type=F32)
        e = jnp.exp(s - jnp.max(s, axis=-1, keepdims=True))
        o = jnp.dot(e.astype(BF16), v_scr[...], preferred_element_type=F32)
        o = o * (1.0 / jnp.sum(e, axis=-1, keepdims=True))
        o_ref[pl.ds(r0, tq), col0:col0 + dv] = o.astype(o_ref.dtype)
        return c

    lax.fori_loop(0, q_scr.shape[0] // tq, body, 0)


def _gqa_kernel(*refs, has_cache, grp):
    if has_cache:
        q_ref, k_ref, v_ref, qn_ref, kn_ref, ck_ref, cv_ref, cos_ref, sin_ref, o_ref, q_scr, k_scr, v_scr = refs
    else:
        q_ref, k_ref, v_ref, qn_ref, kn_ref, o_ref, kout_ref, q_scr, k_scr, v_scr = refs
    l = k_ref.shape[0]
    k = _rms(k_ref[...], kn_ref[...])
    if has_cache:
        p = ck_ref.shape[0]
        k_scr[0:p, :] = ck_ref[...].astype(BF16)
        v_scr[0:p, :] = cv_ref[...].astype(BF16)
        k_scr[p:p + l, :] = _rope(k, cos_ref[...], sin_ref[...], GQA_HD).astype(BF16)
        v_scr[p:p + l, :] = v_ref[...].astype(BF16)
    else:
        kout_ref[...] = k
        k_scr[...] = k.astype(BF16)
        v_scr[...] = v_ref[...].astype(BF16)
    scale = GQA_HD ** -0.5
    for g in range(grp):
        q = _rms(q_ref[:, g * GQA_HD:(g + 1) * GQA_HD], qn_ref[...])
        if has_cache:
            q = _rope(q, cos_ref[...], sin_ref[...], GQA_HD)
        q_scr[...] = (q * scale).astype(BF16)
        _attend(q_scr, k_scr, v_scr, o_ref, g * GQA_HD, _tile(l, 256))


def _gqa(xin, cols, qn, kn, b, l, row_blk0, cache=None, layer=0):
    q0, k0, v0 = cols
    n_q = k0 - q0
    grp = n_q // GQA_KV // GQA_HD
    qw = grp * GQA_HD
    has_cache = cache is not None
    in_specs = [pl.BlockSpec((l, qw), lambda i, h: (row_blk0 + i, q0 // qw + h)),
                pl.BlockSpec((l, GQA_HD), lambda i, h: (row_blk0 + i, k0 // GQA_HD + h)),
                pl.BlockSpec((l, GQA_HD), lambda i, h: (row_blk0 + i, v0 // GQA_HD + h)),
                pl.BlockSpec((1, GQA_HD), lambda i, h: (0, 0)),
                pl.BlockSpec((1, GQA_HD), lambda i, h: (0, 0))]
    args = [xin, xin, xin, qn.reshape(1, GQA_HD), kn.reshape(1, GQA_HD)]
    lk = l
    o_shape = jax.ShapeDtypeStruct((b * l, n_q), BF16)
    o_spec = pl.BlockSpec((l, qw), lambda i, h: (i, h))
    if has_cache:
        ck, cv = cache
        p = ck.shape[2]
        lk = l + p
        cspec = pl.BlockSpec((None, None, p, GQA_HD), lambda i, h: (i, layer, 0, h))
        tspec = pl.BlockSpec((l, LANES), lambda i, h: (0, 0))
        cos, sin = _rope_tables(l, GQA_HD)
        in_specs += [cspec, cspec, tspec, tspec]
        args += [ck, cv, cos, sin]
        out_shape, out_specs = o_shape, o_spec
    else:
        out_shape = (o_shape, jax.ShapeDtypeStruct((b * l, GQA_KV * GQA_HD), F32))
        out_specs = (o_spec, pl.BlockSpec((l, GQA_HD), lambda i, h: (i, h)))
    return pl.pallas_call(
        functools.partial(_gqa_kernel, has_cache=has_cache, grp=grp), name="gqa_lat" if has_cache else "gqa_ctx",
        out_shape=out_shape, grid=(b, GQA_KV), in_specs=in_specs, out_specs=out_specs,
        scratch_shapes=[pltpu.VMEM((l, GQA_HD), BF16), pltpu.VMEM((lk, GQA_HD), BF16), pltpu.VMEM((lk, GQA_HD), BF16)],
        compiler_params=_cparams(2),
    )(*args)


def _mla_kernel(*refs, has_cache, heads):
    if has_cache:
        (qc_ref, ckv_ref, kr_ref, qn_ref, kvn_ref, wqn_ref, wqr_ref, wuk_ref, wuv_ref, cc_ref, ckr_ref, cos_ref,
         sin_ref, o_ref, qn_scr, qr_scr, kn_scr, vv_scr, kr_scr) = refs
    else:
        (qc_ref, ckv_ref, kr_ref, qn_ref, kvn_ref, wqn_ref, wqr_ref, wuk_ref, wuv_ref, o_ref, ckv_out_ref,
         qn_scr, qr_scr, kn_scr, vv_scr, kr_scr) = refs
    l = qc_ref.shape[0]
    scale = (MLA_NOPE + MLA_ROPE) ** -0.5
    qc = _rms(qc_ref[...], qn_ref[...]).astype(BF16)
    qn_scr[...] = jnp.dot(qc, wqn_ref[...], preferred_element_type=F32) * scale
    qr = jnp.dot(qc, wqr_ref[...], preferred_element_type=F32) * scale
    ckv = _rms(ckv_ref[...], kvn_ref[...])
    kr = kr_ref[...]
    lane = lax.broadcasted_iota(I32, kr.shape, 1)
    kr = jnp.where(lane < MLA_ROPE, kr, pltpu.roll(kr, MLA_ROPE, 1))
    if has_cache:
        p = cc_ref.shape[0]
        cos, sin = cos_ref[...], sin_ref[...]
        for c in range(qr.shape[1] // LANES):
            qr_scr[:, c * LANES:(c + 1) * LANES] = _rope(qr[:, c * LANES:(c + 1) * LANES], cos, sin, MLA_ROPE)
        kr_scr[0:p, :] = ckr_ref[...].astype(BF16)
        kr_scr[p:p + l, :] = _rope(kr, cos, sin, MLA_ROPE).astype(BF16)
        cc = cc_ref[...].astype(BF16)
        kn_scr[0:p, :] = jnp.dot(cc, wuk_ref[...], preferred_element_type=F32).astype(BF16)
        vv_scr[0:p, :] = jnp.dot(cc, wuv_ref[...], preferred_element_type=F32).astype(BF16)
    else:
        p = 0
        qr_scr[...] = qr
        kr_scr[...] = kr.astype(BF16)
        ckv_out_ref[...] = ckv
    cb = ckv.astype(BF16)
    kn_scr[p:p + l, :] = jnp.dot(cb, wuk_ref[...], preferred_element_type=F32).astype(BF16)
    vv_scr[p:p + l, :] = jnp.dot(cb, wuv_ref[...], preferred_element_type=F32).astype(BF16)
    tq = _tile(l, 256)
    nt = (((1,), (1,)), ((), ()))
    for h in range(heads):
        c = (h * MLA_ROPE) // LANES

        def tile(t, carry, h=h, c=c):
            rows = slice(0, tq) if l == tq else pl.ds(pl.multiple_of(t * tq, tq), tq)
            qrh = qr_scr[rows, c * LANES:(c + 1) * LANES]
            lq = lax.broadcasted_iota(I32, qrh.shape, 1)
            own = (lq // MLA_ROPE) == (h % (LANES // MLA_ROPE))
            s = lax.dot_general(qn_scr[rows, h * MLA_NOPE:(h + 1) * MLA_NOPE].astype(BF16),
                                kn_scr[:, h * MLA_NOPE:(h + 1) * MLA_NOPE], nt, preferred_element_type=F32)
            s += lax.dot_general(jnp.where(own, qrh, 0.0).astype(BF16), kr_scr[...], nt, preferred_element_type=F32)
            e = jnp.exp(s - jnp.max(s, axis=-1, keepdims=True))
            o = jnp.dot(e.astype(BF16), vv_scr[:, h * MLA_V:(h + 1) * MLA_V], preferred_element_type=F32)
            o = o * (1.0 / jnp.sum(e, axis=-1, keepdims=True))
            o_ref[rows, h * MLA_V:(h + 1) * MLA_V] = o.astype(o_ref.dtype)
            return carry

        if l == tq:
            tile(0, 0)
        else:
            lax.fori_loop(0, l // tq, tile, 0)


def _mla(xin, col0, qn, kvn, w_uq, w_ukv, b, l, row_blk0, cache=None, layer=0):
    heads = w_uq.shape[1] // (MLA_NOPE + MLA_ROPE)
    wq = w_uq.reshape(MLA_Q_LORA, heads, MLA_NOPE + MLA_ROPE)
    wqn = wq[:, :, :MLA_NOPE].reshape(MLA_Q_LORA, heads * MLA_NOPE).astype(BF16)
    wqr = wq[:, :, MLA_NOPE:].reshape(MLA_Q_LORA, heads * MLA_ROPE).astype(BF16)
    wkv = w_ukv.reshape(MLA_KV_LORA, heads, MLA_NOPE + MLA_V)
    wuk = wkv[:, :, :MLA_NOPE].reshape(MLA_KV_LORA, heads * MLA_NOPE).astype(BF16)
    wuv = wkv[:, :, MLA_NOPE:].reshape(MLA_KV_LORA, heads * MLA_V).astype(BF16)
    has_cache = cache is not None
    c_kv = col0 + MLA_Q_LORA
    c_kr = c_kv + MLA_KV_LORA
    full = lambda a: pl.BlockSpec(a.shape, lambda i: (0,) * a.ndim)
    in_specs = [pl.BlockSpec((l, MLA_Q_LORA), lambda i: (row_blk0 + i, col0 // MLA_Q_LORA)),
                pl.BlockSpec((l, MLA_KV_LORA), lambda i: (row_blk0 + i, c_kv // MLA_KV_LORA)),
                pl.BlockSpec((l, LANES), lambda i: (row_blk0 + i, c_kr // LANES)),
                pl.BlockSpec((1, MLA_Q_LORA), lambda i: (0, 0)),
                pl.BlockSpec((1, MLA_KV_LORA), lambda i: (0, 0)),
                full(wqn), full(wqr), full(wuk), full(wuv)]
    args = [xin, xin, xin, qn.reshape(1, -1), kvn.reshape(1, -1), wqn, wqr, wuk, wuv]
    o_shape = jax.ShapeDtypeStruct((b * l, heads * MLA_V), BF16)
    o_spec = pl.BlockSpec((l, heads * MLA_V), lambda i: (i, 0))
    lk = l
    if has_cache:
        cc, ckr2 = cache
        p = cc.shape[2]
        lk = l + p
        cos, sin = _rope_tables(l, MLA_ROPE)
        in_specs += [pl.BlockSpec((None, None, p, MLA_KV_LORA), lambda i: (i, layer, 0, 0)),
                     pl.BlockSpec((None, None, p, LANES), lambda i: (i, layer, 0, 0)),
                     pl.BlockSpec((l, LANES), lambda i: (0, 0)), pl.BlockSpec((l, LANES), lambda i: (0, 0))]
        args += [cc, ckr2, cos, sin]
        out_shape, out_specs = o_shape, o_spec
    else:
        out_shape = (o_shape, jax.ShapeDtypeStruct((b * l, MLA_KV_LORA), F32))
        out_specs = (o_spec, pl.BlockSpec((l, MLA_KV_LORA), lambda i: (i, 0)))
    scratch = [pltpu.VMEM((l, heads * MLA_NOPE), F32), pltpu.VMEM((l, heads * MLA_ROPE), F32),
               pltpu.VMEM((lk, heads * MLA_NOPE), BF16), pltpu.VMEM((lk, heads * MLA_V), BF16),
               pltpu.VMEM((lk, LANES), BF16)]
    return pl.pallas_call(
        functools.partial(_mla_kernel, has_cache=has_cache, heads=heads), name="mla_lat" if has_cache else "mla_ctx",
        out_shape=out_shape, grid=(b,), in_specs=in_specs, out_specs=out_specs, scratch_shapes=scratch,
        compiler_params=_cparams(1),
    )(*args)


def _cmul(ar, ai, br, bi):
    return ar * br - ai * bi, ar * bi + ai * br


def _s5_operators(lam_re, lam_im, log_dt, b_re, b_im, c_re, c_im, d_skip):
    g, p = lam_re.shape[1:]
    t = SSM_T
    n = jnp.arange(t + 1, dtype=F32)[:, None, None]
    us, ws, ks, a_rows = [], [], [], []
    for d in range(2):
        lr, li = lam_re[d], lam_im[d]
        dt = jnp.exp(log_dt[d])[:, None]
        pw_re = jnp.exp(n * (lr * dt)[None]) * jnp.cos(n * (li * dt)[None])
        pw_im = jnp.exp(n * (lr * dt)[None]) * jnp.sin(n * (li * dt)[None])
        den = lr * lr + li * li
        cf_re, cf_im = _cmul(pw_re[1] - 1.0, pw_im[1], lr / den, -li / den)
        bb_re, bb_im = _cmul(cf_re[..., None], cf_im[..., None], b_re[d], b_im[d])
        cl_re, cl_im = _cmul(c_re[None], c_im[None], pw_re[:t, :, None, :], pw_im[:t, :, None, :])
        ks.append(jnp.einsum("ngcp,gpe->ngce", cl_re, bb_re, precision=HIGHEST)
                  - jnp.einsum("ngcp,gpe->ngce", cl_im, bb_im, precision=HIGHEST))
        e = jnp.arange(t)[::-1] if d == 0 else jnp.arange(t)
        u_re, u_im = _cmul(pw_re[e][:, :, None, :], pw_im[e][:, :, None, :],
                           jnp.swapaxes(bb_re, 1, 2)[None], jnp.swapaxes(bb_im, 1, 2)[None])
        us += [jnp.moveaxis(u_re, 1, 0).reshape(g, t * SSM_CH, p), jnp.moveaxis(u_im, 1, 0).reshape(g, t * SSM_CH, p)]
        e = jnp.arange(1, t + 1) if d == 0 else jnp.arange(t, 0, -1)
        w_re, w_im = _cmul(c_re[None], c_im[None], pw_re[e][:, :, None, :], pw_im[e][:, :, None, :])
        to_w = lambda a: jnp.transpose(a, (1, 3, 0, 2)).reshape(g, p, t * SSM_CH)
        ws += [to_w(w_re), -to_w(w_im)]
        a_rows += [jnp.concatenate([pw_re[t], pw_re[t]], -1), jnp.concatenate([-pw_im[t], pw_im[t]], -1)]
    s_in = jnp.arange(t)[:, None]
    s_out = jnp.arange(t)[None, :]
    kf = jnp.where((s_out >= s_in)[..., None, None, None], ks[0][jnp.clip(s_out - s_in, 0, t - 1)], 0.0)
    kb = jnp.where((s_in >= s_out)[..., None, None, None], ks[1][jnp.clip(s_in - s_out, 0, t - 1)], 0.0)
    skip = (s_in == s_out)[..., None, None, None] * (d_skip.reshape(g, SSM_CH)[None, None, :, :, None]
                                                     * jnp.eye(SSM_CH, dtype=F32)[None, None, None])
    tg = jnp.transpose(kf + kb + skip, (2, 0, 4, 1, 3)).reshape(g, t * SSM_CH, t * SSM_CH)
    u_all = jnp.concatenate(us, axis=-1)
    w_all = jnp.concatenate([tg] + ws, axis=1)
    a_all = jnp.stack(a_rows + a_rows, axis=1)
    return u_all, w_all, a_all


def _gelu_tanh(x):
    return 0.5 * x * (1.0 + jnp.tanh(0.7978845608028654 * (x + 0.044715 * (x * x * x))))


def _s5_kernel(x_ref, u_ref, w_ref, a_ref, h0_ref, y_ref, fin_ref, v_scr, hp_scr, gn_scr, *, nb):
    rows = x_ref.shape[0]
    nj = rows // nb
    x = x_ref[...]
    v_scr[...] = jnp.dot(x, u_ref[...], precision=HIGHEST, preferred_element_type=F32)
    two_p = a_ref.shape[1]
    a = a_ref[...]

    def fwd(j, h):
        r0 = pl.multiple_of(j * nb, nb)
        hp_scr[pl.ds(r0, nb), :] = h
        return a[0:1] * h + a[1:2] * pltpu.roll(h, two_p // 2, 1) + v_scr[pl.ds(r0, nb), 0:two_p]

    h_fin = lax.fori_loop(0, nj, fwd, h0_ref[0])

    def bwd(i, h):
        r0 = pl.multiple_of((nj - 1 - i) * nb, nb)
        gn_scr[pl.ds(r0, nb), :] = h
        return a[2:3] * h + a[3:4] * pltpu.roll(h, two_p // 2, 1) + v_scr[pl.ds(r0, nb), two_p:2 * two_p]

    g_fin = lax.fori_loop(0, nj, bwd, h0_ref[1])
    fin_ref[0] = h_fin
    fin_ref[1] = g_fin
    tc = x.shape[1]
    y = jnp.dot(x, w_ref[0:tc, :], precision=HIGHEST, preferred_element_type=F32)
    y += jnp.dot(hp_scr[...], w_ref[tc:tc + two_p, :], precision=HIGHEST, preferred_element_type=F32)
    y += jnp.dot(gn_scr[...], w_ref[tc + two_p:tc + 2 * two_p, :], precision=HIGHEST, preferred_element_type=F32)
    y_ref[...] = _gelu_tanh(y)


def _s5_scan(xg, u_all, w_all, a_all, h0, nb):
    g, rows, tc = xg.shape
    two_p = a_all.shape[2]
    blk = lambda a: pl.BlockSpec((None,) + a.shape[1:], lambda i: (i,) + (0,) * (a.ndim - 1))
    return pl.pallas_call(
        functools.partial(_s5_kernel, nb=nb), name="s5",
        out_shape=(jax.ShapeDtypeStruct((g, rows, tc), F32), jax.ShapeDtypeStruct((g, 2, nb, two_p), F32)),
        grid=(g,),
        in_specs=[blk(xg), blk(u_all), blk(w_all), blk(a_all), blk(h0)],
        out_specs=(pl.BlockSpec((None, rows, tc), lambda i: (i, 0, 0)),
                   pl.BlockSpec((None, 2, nb, two_p), lambda i: (i, 0, 0, 0))),
        scratch_shapes=[pltpu.VMEM((rows, 2 * two_p), F32), pltpu.VMEM((rows, two_p), F32),
                        pltpu.VMEM((rows, two_p), F32)],
        compiler_params=_cparams(1),
    )(xg, u_all, w_all, a_all, h0)


def _glu_kernel(y_ref, w_ref, b_ref, o_ref):
    y = y_ref[...]
    z = jnp.dot(y.astype(BF16), w_ref[...], preferred_element_type=F32) + b_ref[...]
    o_ref[...] = (y * _sigmoid(z)).astype(o_ref.dtype)


def _glu(y, w, b):
    m, n = y.shape
    tm = _tile(m, 512)
    return pl.pallas_call(
        _glu_kernel, name="s5_glu",
        out_shape=jax.ShapeDtypeStruct((m, n), BF16),
        grid=(m // tm,),
        in_specs=[pl.BlockSpec((tm, n), lambda i: (i, 0)), pl.BlockSpec((n, n), lambda i: (0, 0)),
                  pl.BlockSpec((1, n), lambda i: (0, 0))],
        out_specs=pl.BlockSpec((tm, n), lambda i: (i, 0)),
        compiler_params=_cparams(1),
    )(y, w.astype(BF16), b.reshape(1, n))


def _s5_path(u, ops, b, l, h0):
    g = ops[0].shape[0]
    nj = l // SSM_T
    xg = u.reshape(b, nj, SSM_T, g, SSM_CH).transpose(3, 1, 0, 2, 4).reshape(g, nj * b, SSM_T * SSM_CH)
    yg, fin = _s5_scan(xg, *ops, h0, b)
    y = yg.reshape(g, nj, b, SSM_T, SSM_CH).transpose(2, 1, 3, 0, 4).reshape(b * l, g * SSM_CH)
    return y, fin


def _softplus(x):
    return jnp.maximum(x, 0.0) + jnp.log1p(jnp.exp(-jnp.abs(x)))


def _dn_conv(x_ref, w_ref):
    x = x_ref[...]
    l = x.shape[0]
    w = w_ref[...]
    t = lax.broadcasted_iota(I32, x.shape, 0)
    pad = (DN_CONV - 1) // 2
    acc = x * w[pad:pad + 1]
    for tau in range(DN_CONV):
        delta = tau - pad
        if delta == 0:
            continue
        shifted = pltpu.roll(x, (-delta) % l, 0)
        ok = (t + delta >= 0) & (t + delta < l)
        acc += jnp.where(ok, shifted, 0.0) * w[tau:tau + 1]
    return _silu(acc)


def _l2(x):
    return x * lax.rsqrt(jnp.sum(x * x, axis=-1, keepdims=True) + EPS)


def _dn_intra(ci, q_scr, k_scr, v_scr, gb_scr, u_scr, w_scr, qd_scr, kd_scr, qk_scr, et_scr):
    c = DN_CHUNK
    hd = DN_HEAD_DIM
    ri = lax.broadcasted_iota(I32, (c, c), 0)
    cj = lax.broadcasted_iota(I32, (c, c), 1)
    eye = jnp.where(ri == cj, 1.0, 0.0)
    incl = [(cj <= ri), (cj >= ri)]
    strict = [(cj < ri), (cj > ri)]
    rows, one = [], []
    for i in ci:
        static = isinstance(i, int)
        rows.append(slice(i * c, (i + 1) * c) if static else pl.ds(pl.multiple_of(i * c, c), c))
        one.append(slice(i, i + 1) if static else pl.ds(i, 1))
    n = len(ci)
    pairs = [(t, d) for t in range(n) for d in range(2)]
    q = [q_scr[r, :] for r in rows]
    k = [k_scr[r, :] for r in rows]
    kk = [_bdot_nt(k[t], k[t]) for t in range(n)]
    qk_raw = [_bdot_nt(q[t], k[t]) for t in range(n)]
    g = [gb_scr[rows[t], 2 * d * hd:(2 * d + 1) * hd] for t, d in pairs]
    beta = [gb_scr[rows[t], (2 * d + 1) * hd:(2 * d + 2) * hd] for t, d in pairs]
    gc = [jnp.dot(incl[d].astype(F32), g[p], precision=HIGHEST, preferred_element_type=F32)
          for p, (t, d) in enumerate(pairs)]
    lane = lax.broadcasted_iota(I32, (c, hd), 1)
    ones = jnp.ones((c, hd), F32)
    gc_row = [lax.dot_general(ones, jnp.where(lane == 0, x, 0.0), (((1,), (1,)), ((), ())),
                              precision=HIGHEST, preferred_element_type=F32) for x in gc]
    dmat = [jnp.exp(jnp.where(incl[d], gc[p][:, 0:c] - gc_row[p], NEG_BIG)) for p, (t, d) in enumerate(pairs)]
    a = [jnp.where(strict[d], beta[p][:, 0:c] * kk[t] * dmat[p], 0.0) for p, (t, d) in enumerate(pairs)]
    inv = [eye - x for x in a]
    pw = [_dot3(x, x) for x in a]
    for r in range(int(math.log2(c)) - 1):
        inv = [x + _dot3(x, y) for x, y in zip(inv, pw)]
        if r + 2 < int(math.log2(c)):
            pw = [_dot3(y, y) for y in pw]
    egc = [jnp.exp(x) for x in gc]
    tot = [gc[p][c - 1:c, :] if d == 0 else gc[p][0:1, :] for p, (t, d) in enumerate(pairs)]
    u = [_dot3(inv[p], v_scr[rows[t], :] * beta[p]) for p, (t, d) in enumerate(pairs)]
    w = [_dot3(inv[p], k[t] * beta[p] * egc[p]) for p, (t, d) in enumerate(pairs)]
    for p, (t, d) in enumerate(pairs):
        u_scr[d, rows[t], :] = u[p]
        w_scr[d, rows[t], :] = w[p]
        qd_scr[d, rows[t], :] = q[t] * egc[p]
        kd_scr[d, rows[t], :] = k[t] * jnp.exp(tot[p] - gc[p])
        qk_scr[d, rows[t], :] = qk_raw[t] * dmat[p]
        et_scr[d, one[t], :] = jnp.exp(tot[p])


def _dn_inter(ci, u_scr, w_scr, qd_scr, kd_scr, qk_scr, et_scr, s_scr, o_scr):
    c = DN_CHUNK
    rows = [pl.ds(pl.multiple_of(i * c, c), c) for i in ci]
    s = [s_scr[d] for d in range(2)]
    ws = [_bdot(jnp.concatenate([w_scr[d, rows[d], :], qd_scr[d, rows[d], :]], axis=0), s[d]) for d in range(2)]
    v_new = [u_scr[d, rows[d], :] - ws[d][0:c] for d in range(2)]
    o = [ws[d][c:2 * c] + _bdot(qk_scr[d, rows[d], :], v_new[d]) for d in range(2)]
    s_new = [s[d] * et_scr[d, pl.ds(ci[d], 1), :] + _bdot_tn(kd_scr[d, rows[d], :], v_new[d]) for d in range(2)]
    for d in range(2):
        o_scr[d, rows[d], :] = o[d]
        s_scr[d] = s_new[d]


def _dn_kernel(*refs, has_state, nh):
    if has_state:
        (q_ref, k_ref, v_ref, z_ref, ab_ref, wq_ref, wk_ref, wv_ref, alog_ref, bias_ref, ng_ref, s0_ref, o_ref,
         q_scr, k_scr, v_scr, gb_scr, s_scr, o_scr, *chunk_scr) = refs
    else:
        (q_ref, k_ref, v_ref, z_ref, ab_ref, wq_ref, wk_ref, wv_ref, alog_ref, bias_ref, ng_ref, o_ref, sfin_ref,
         q_scr, k_scr, v_scr, gb_scr, s_scr, o_scr, *chunk_scr) = refs
    hd = DN_HEAD_DIM
    h = pl.program_id(1)
    l = q_ref.shape[0]
    n = l // DN_CHUNK
    q_scr[...] = _l2(_dn_conv(q_ref, wq_ref)) * hd ** -0.5
    k_scr[...] = _l2(_dn_conv(k_ref, wk_ref))
    v_scr[...] = _dn_conv(v_ref, wv_ref)
    x = ab_ref[...]
    lane = lax.broadcasted_iota(I32, x.shape, 1)
    is_decay = ((lane // nh) % 2) == 0
    mix = jnp.where(is_decay, -jnp.exp(alog_ref[...]) * _softplus(x + bias_ref[...]), _sigmoid(x))
    sr = lax.broadcasted_iota(I32, (LANES, 4 * hd), 0)
    sc = lax.broadcasted_iota(I32, (LANES, 4 * hd), 1)
    sel = (sr == LANES // 2 + nh * (sc // hd) + h).astype(F32)
    gb_scr[...] = jnp.dot(mix, sel, precision=HIGHEST, preferred_element_type=F32)
    if has_state:
        s_scr[...] = s0_ref[...]
    else:
        s_scr[...] = jnp.zeros(s_scr.shape, F32)

    per_body = min(n, 4)

    def intra(i, carry):
        _dn_intra([i * per_body + t for t in range(per_body)], q_scr, k_scr, v_scr, gb_scr, *chunk_scr)
        return carry

    if n == per_body:
        intra(0, 0)
    else:
        lax.fori_loop(0, n // per_body, intra, 0)

    def inter(c, carry):
        _dn_inter([c, n - 1 - c], *chunk_scr, s_scr, o_scr)
        return carry

    lax.fori_loop(0, n, inter, 0)
    o = _rms(o_scr[0] + o_scr[1], ng_ref[...]) * _silu(z_ref[...])
    o_ref[...] = o.astype(o_ref.dtype)
    if not has_state:
        sfin_ref[...] = s_scr[...]


def _deltanet(xin, ab_col, conv_w, a_log, dt_bias, norm_g, b, l, row_blk0, state=None, layer=0):
    hd = DN_HEAD_DIM
    nh = a_log.shape[1]
    has_state = state is not None
    lane_par = lambda p: jnp.zeros((1, LANES), F32).at[0, LANES // 2:LANES // 2 + 4 * nh].set(
        jnp.concatenate([p[0], jnp.zeros_like(p[0]), p[1], jnp.zeros_like(p[1])]))
    xspec = lambda sec: pl.BlockSpec((l, hd), lambda i, h: (row_blk0 + i, sec * nh + h))
    wspec = lambda sec: pl.BlockSpec((DN_CONV, hd), lambda i, h: (0, sec * nh + h))
    vec = pl.BlockSpec((1, LANES), lambda i, h: (0, 0))
    in_specs = [xspec(0), xspec(1), xspec(2), xspec(3),
                pl.BlockSpec((l, LANES), lambda i, h: (row_blk0 + i, ab_col // LANES)),
                wspec(0), wspec(1), wspec(2), vec, vec, vec]
    args = [xin, xin, xin, xin, xin, conv_w, conv_w, conv_w, lane_par(a_log), lane_par(dt_bias),
            norm_g.reshape(1, hd)]
    o_shape = jax.ShapeDtypeStruct((b * l, nh * hd), BF16)
    o_spec = pl.BlockSpec((l, hd), lambda i, h: (i, h))
    if has_state:
        in_specs.append(pl.BlockSpec((None, None, 2, None, hd, hd), lambda i, h: (i, layer, 0, h, 0, 0)))
        args.append(state)
        out_shape, out_specs = o_shape, o_spec
    else:
        out_shape = (o_shape, jax.ShapeDtypeStruct((b, 2, nh, hd, hd), F32))
        out_specs = (o_spec, pl.BlockSpec((None, 2, None, hd, hd), lambda i, h: (i, 0, h, 0, 0)))
    scratch = [pltpu.VMEM((l, hd), F32), pltpu.VMEM((l, hd), F32), pltpu.VMEM((l, hd), F32),
               pltpu.VMEM((l, 4 * hd), F32), pltpu.VMEM((2, hd, hd), F32), pltpu.VMEM((2, l, hd), F32)]
    scratch += [pltpu.VMEM((2, l, hd), F32)] * 4 + [pltpu.VMEM((2, l, DN_CHUNK), F32),
                                                    pltpu.VMEM((2, l // DN_CHUNK, hd), F32)]
    return pl.pallas_call(
        functools.partial(_dn_kernel, has_state=has_state, nh=nh), name="deltanet_lat" if has_state else "deltanet_ctx",
        out_shape=out_shape, grid=(b, nh), in_specs=in_specs, out_specs=out_specs, scratch_shapes=scratch,
        compiler_params=_cparams(2),
    )(*args)


def _round_up(n, k):
    return -(-n // k) * k


def kernel(x_prompt, x_sample, cache_gqa_k, cache_gqa_v, cache_mla_ckv, cache_mla_krope, state_dn, state_ssm_re, state_ssm_im, c, c_ctx, w_ada, b_ada, norm1_g, norm2_g, w_in, dn_conv, dn_a_log, dn_dt_bias, dn_norm_g, ssm_lam_re, ssm_lam_im, ssm_log_dt, ssm_b_re, ssm_b_im, ssm_c_re, ssm_c_im, ssm_d, ssm_w_glu, ssm_b_glu, gqa_q_norm, gqa_k_norm, mla_q_norm, mla_kv_norm, mla_w_uq, mla_w_ukv, w_out, moe_w_router, moe_b_router, moe_w_gate, moe_b_gate, moe_w_up, moe_b_up, moe_w_down, moe_b_down, norm_f):
    bc, lc, d = x_prompt.shape
    bl, ll, _ = x_sample.shape
    depth = w_in.shape[0]
    m_ctx = bc * lc
    assert m_ctx % ll == 0 and lc % DN_CHUNK == 0 and ll % DN_CHUNK == 0
    lat_blk0 = m_ctx // ll
    gw = d // 4
    nh = dn_a_log.shape[2]
    n_gqa = gw + 2 * GQA_KV * GQA_HD
    n_mla = MLA_Q_LORA + MLA_KV_LORA + MLA_ROPE
    dn_in = 4 * gw + 4 * nh
    assert w_in.shape[2] == dn_in + gw + n_gqa + n_mla

    c_ssm = 4 * gw
    c_gqa = c_ssm + gw
    q0, k0, v0 = c_gqa, c_gqa + gw, c_gqa + gw + GQA_KV * GQA_HD
    c_mla = _round_up(c_gqa + n_gqa, MLA_Q_LORA)
    c_kr = c_mla + MLA_Q_LORA + MLA_KV_LORA
    assert (c_mla + MLA_Q_LORA) % MLA_KV_LORA == 0 and c_kr % LANES == 0 and MLA_ROPE + 4 * nh <= LANES
    n_w = _round_up(c_kr + LANES, 512)
    o_ssm, o_gqa, o_mla = dn_in, dn_in + gw, dn_in + gw + n_gqa
    zpad = lambda n: jnp.zeros((depth, d, n), F32)
    w_in_p = jnp.concatenate(
        [w_in[:, :, :4 * gw], w_in[:, :, o_ssm:o_gqa], w_in[:, :, o_gqa:o_mla], zpad(c_mla - c_gqa - n_gqa),
         w_in[:, :, o_mla:o_mla + MLA_Q_LORA + MLA_KV_LORA],
         w_in[:, :, o_mla + MLA_Q_LORA + MLA_KV_LORA:], zpad(LANES // 2 - MLA_ROPE), w_in[:, :, 4 * gw:dn_in],
         zpad(n_w - c_kr - LANES // 2 - 4 * nh)], axis=-1).astype(BF16)
    w_out_b = w_out.astype(BF16)

    x = jnp.concatenate([x_prompt.reshape(m_ctx, d), x_sample.reshape(bl * ll, d)], axis=0)
    n_cond = _round_up(1 + bl, 8)
    cond = jnp.zeros((n_cond, d), F32).at[0].set(c_ctx).at[1:1 + bl].set(c)

    p_len = cache_gqa_k.shape[2]
    ck = cache_gqa_k.reshape(bl, depth, p_len, GQA_KV * GQA_HD)
    cv = cache_gqa_v.reshape(bl, depth, p_len, GQA_KV * GQA_HD)
    ckr2 = jnp.concatenate([cache_mla_krope, cache_mla_krope], axis=-1)
    n_grp, n_p = ssm_lam_re.shape[2:]

    outs = [[] for _ in range(7)]
    for l in range(depth):
        mod = _ada(cond, w_ada, b_ada[l], l).reshape(n_cond, 6, 1, d)
        xin = _inproj(x, mod, norm1_g[l], w_in_p, l, m_ctx, ll)

        dn_args = (xin, c_kr, dn_conv[l], dn_a_log[l], dn_dt_bias[l], dn_norm_g[l])
        o_dn_c, dn_fin = _deltanet(*dn_args, bc, lc, 0)
        o_dn_l = _deltanet(*dn_args, bl, ll, lat_blk0, state=state_dn, layer=l)

        ops = _s5_operators(ssm_lam_re[l], ssm_lam_im[l], ssm_log_dt[l], ssm_b_re[l], ssm_b_im[l],
                            ssm_c_re[l], ssm_c_im[l], ssm_d[l])
        u = xin[:, c_ssm:c_ssm + gw]
        y_c, ssm_fin = _s5_path(u[:m_ctx], ops, bc, lc, jnp.zeros((n_grp, 2, bc, 2 * n_p), F32))
        h0 = jnp.concatenate([state_ssm_re[:, l], state_ssm_im[:, l]], axis=-1).transpose(2, 1, 0, 3)
        y_l, _ = _s5_path(u[m_ctx:], ops, bl, ll, h0)
        o_s5 = _glu(jnp.concatenate([y_c, y_l], axis=0), ssm_w_glu[l], ssm_b_glu[l])

        gqa_args = (xin, (q0, k0, v0), gqa_q_norm[l], gqa_k_norm[l])
        o_gqa_c, k_new = _gqa(*gqa_args, bc, lc, 0)
        o_gqa_l = _gqa(*gqa_args, bl, ll, lat_blk0, cache=(ck, cv), layer=l)

        mla_args = (xin, c_mla, mla_q_norm[l], mla_kv_norm[l], mla_w_uq[l], mla_w_ukv[l])
        o_mla_c, ckv_new = _mla(*mla_args, bc, lc, 0)
        o_mla_l = _mla(*mla_args, bl, ll, lat_blk0, cache=(cache_mla_ckv, ckr2), layer=l)

        cat = lambda a, b_: jnp.concatenate([a, b_], axis=0)
        parts = [cat(o_dn_c, o_dn_l), o_s5, cat(o_gqa_c, o_gqa_l), cat(o_mla_c, o_mla_l)]
        x = _outproj(parts, w_out_b, l, x, mod, m_ctx, ll)
        x = _moe(x, mod, norm2_g[l], moe_w_router[l], moe_b_router[l], moe_w_gate, moe_b_gate,
                 moe_w_up, moe_b_up, moe_w_down, moe_b_down, l, norm_f, m_ctx, ll,
                 final_norm=(l == depth - 1))

        fin = ssm_fin.transpose(2, 1, 0, 3)
        outs[0].append(k_new.reshape(bc, lc, GQA_KV, GQA_HD))
        outs[1].append(xin[:m_ctx, v0:v0 + GQA_KV * GQA_HD].reshape(bc, lc, GQA_KV, GQA_HD))
        outs[2].append(ckv_new.reshape(bc, lc, MLA_KV_LORA))
        outs[3].append(xin[:m_ctx, c_kr:c_kr + MLA_ROPE].reshape(bc, lc, MLA_ROPE))
        outs[4].append(dn_fin)
        outs[5].append(fin[..., :n_p])
        outs[6].append(fin[..., n_p:])

    y_prompt = x[:m_ctx].reshape(bc, lc, d)
    y_sample = x[m_ctx:].reshape(bl, ll, d)
    return (y_prompt, y_sample) + tuple(jnp.stack(o, axis=1) for o in outs)
```

```python
import functools
import math

import jax
import jax.numpy as jnp
from jax import lax
from jax.experimental import pallas as pl
from jax.experimental.pallas import tpu as pltpu

F32 = jnp.float32
BF16 = jnp.bfloat16
I32 = jnp.int32

EPS = 1e-6
ROPE_THETA = 10000.0
GRID_W = 64
N_EXPERTS = 32
TOP_K = 4
SWIGLU_LIMIT = 7.0
SWIGLU_ALPHA = 1.702
DN_CONV = 5
DN_CHUNK = 64
DN_HEAD_DIM = 128
SSM_CH = 16
SSM_P = 64
SSM_T = 16
GQA_HD = 128
GQA_KV = 2
MLA_NOPE = 128
MLA_ROPE = 64
MLA_V = 128
MLA_Q_LORA = 768
MLA_KV_LORA = 256

LANES = 128
VMEM_LIMIT = 56 * 1024 * 1024
NEG_BIG = -1e30
HIGHEST = lax.Precision.HIGHEST


def _cparams(n_axes):
    return pltpu.CompilerParams(dimension_semantics=("arbitrary",) * n_axes, vmem_limit_bytes=VMEM_LIMIT)


def _tile(n, want):
    t = min(n, want)
    while n % t:
        t //= 2
    return t


def _bdot(a, b):
    return jnp.dot(a.astype(BF16), b.astype(BF16), preferred_element_type=F32)


def _bdot_nt(a, b):
    return lax.dot_general(a.astype(BF16), b.astype(BF16), (((1,), (1,)), ((), ())), preferred_element_type=F32)


def _bdot_tn(a, b):
    return lax.dot_general(a.astype(BF16), b.astype(BF16), (((0,), (0,)), ((), ())), preferred_element_type=F32)


def _split(a):
    hi = a.astype(BF16)
    lo = (a - hi.astype(F32)).astype(BF16)
    return hi, lo


def _dot3(a, b):
    ah, al = _split(a)
    bh, bl = _split(b)
    d = functools.partial(jnp.dot, preferred_element_type=F32)
    return d(ah, bh) + (d(ah, bl) + d(al, bh))


def _sigmoid(x):
    return 1.0 / (1.0 + jnp.exp(-x))


def _silu(x):
    return x * _sigmoid(x)


def _rms(x, g):
    return x * lax.rsqrt(jnp.mean(x * x, axis=-1, keepdims=True) + EPS) * g


def _ada_kernel(c_ref, w_ref, b_ref, o_ref):
    o_ref[...] = _bdot(_silu(c_ref[...]), w_ref[...]) + b_ref[...]


def _ada(cond, w, b, layer):
    r, d = cond.shape
    n = w.shape[2]
    tn = _tile(n, 512)
    return pl.pallas_call(
        _ada_kernel, name="ada",
        out_shape=jax.ShapeDtypeStruct((r, n), F32),
        grid=(n // tn,),
        in_specs=[pl.BlockSpec((r, d), lambda j: (0, 0)),
                  pl.BlockSpec((None, d, tn), lambda j: (layer, 0, j)),
                  pl.BlockSpec((1, tn), lambda j: (0, j))],
        out_specs=pl.BlockSpec((r, tn), lambda j: (0, j)),
        compiler_params=_cparams(1),
    )(cond, w, b.reshape(1, n))


def _mod_row(i, tm, m_ctx, l_lat):
    r0 = i * tm
    return jnp.where(r0 < m_ctx, 0, 1 + (r0 - m_ctx) // l_lat)


def _mod_spec(d, which, tm, m_ctx, l_lat, n_grid):
    if n_grid == 1:
        return pl.BlockSpec((None, None, 1, d), lambda i: (_mod_row(i, tm, m_ctx, l_lat), which, 0, 0))
    return pl.BlockSpec((None, None, 1, d), lambda i, j: (_mod_row(i, tm, m_ctx, l_lat), which, 0, 0))


def _inproj_kernel(x_ref, sh_ref, sc_ref, g_ref, w_ref, o_ref, h_ref):
    @pl.when(pl.program_id(1) == 0)
    def _():
        y = _rms(x_ref[...], g_ref[...])
        h_ref[...] = (y * (1.0 + sc_ref[...]) + sh_ref[...]).astype(BF16)

    o_ref[...] = jnp.dot(h_ref[...], w_ref[...], preferred_element_type=F32)


def _inproj(x, mod4, g, w, layer, m_ctx, l_lat):
    m, d = x.shape
    n = w.shape[2]
    tm = _tile(math.gcd(m_ctx, l_lat), 512)
    tn = _tile(n, 512)
    if n % tn or tn % LANES:
        tn = LANES
    return pl.pallas_call(
        _inproj_kernel, name="inproj",
        out_shape=jax.ShapeDtypeStruct((m, n), F32),
        grid=(m // tm, n // tn),
        in_specs=[pl.BlockSpec((tm, d), lambda i, j: (i, 0)),
                  _mod_spec(d, 0, tm, m_ctx, l_lat, 2),
                  _mod_spec(d, 1, tm, m_ctx, l_lat, 2),
                  pl.BlockSpec((1, d), lambda i, j: (0, 0)),
                  pl.BlockSpec((None, d, tn), lambda i, j: (layer, 0, j))],
        out_specs=pl.BlockSpec((tm, tn), lambda i, j: (i, j)),
        scratch_shapes=[pltpu.VMEM((tm, d), BF16)],
        compiler_params=_cparams(2),
    )(x, mod4, mod4, g.reshape(1, d), w)


def _outproj_kernel(a0, a1, a2, a3, w_ref, x_ref, g_ref, o_ref):
    gw = a0.shape[1]
    acc = jnp.dot(a0[...], w_ref[0:gw, :], preferred_element_type=F32)
    acc += jnp.dot(a1[...], w_ref[gw:2 * gw, :], preferred_element_type=F32)
    acc += jnp.dot(a2[...], w_ref[2 * gw:3 * gw, :], preferred_element_type=F32)
    acc += jnp.dot(a3[...], w_ref[3 * gw:4 * gw, :], preferred_element_type=F32)
    o_ref[...] = x_ref[...] + g_ref[...] * acc


def _outproj(parts, w, layer, x, mod4, m_ctx, l_lat):
    m, d = x.shape
    gw = parts[0].shape[1]
    tm = _tile(math.gcd(m_ctx, l_lat), 512)
    tn = _tile(d, 512)
    part_spec = pl.BlockSpec((tm, gw), lambda i, j: (i, 0))
    g_spec = pl.BlockSpec((None, None, 1, tn), lambda i, j: (_mod_row(i, tm, m_ctx, l_lat), 2, 0, j))
    return pl.pallas_call(
        _outproj_kernel, name="outproj",
        out_shape=jax.ShapeDtypeStruct((m, d), F32),
        grid=(m // tm, d // tn),
        in_specs=[part_spec] * 4 + [pl.BlockSpec((None, 4 * gw, tn), lambda i, j: (layer, 0, j)),
                                    pl.BlockSpec((tm, tn), lambda i, j: (i, j)),
                                    g_spec],
        out_specs=pl.BlockSpec((tm, tn), lambda i, j: (i, j)),
        compiler_params=_cparams(2),
    )(*parts, w, x, mod4)


def _router_kernel(x_ref, sh_ref, sc_ref, g_ref, wr_ref, br_ref, h_ref, idx_ref, gate_ref):
    y = _rms(x_ref[...], g_ref[...])
    h = y * (1.0 + sc_ref[...]) + sh_ref[...]
    half = h.shape[1] // 2
    bits = pltpu.bitcast(h.astype(BF16).astype(F32), jnp.uint32)
    packed = bits[:, half:] | lax.shift_right_logical(bits[:, :half], jnp.uint32(16))
    for c in range(half // LANES):
        h_ref[:, c, :] = packed[:, c * LANES:(c + 1) * LANES]
    logits =jnp.dot(h, wr_ref[...], precision=HIGHEST, preferred_element_type=F32) + br_ref[...]
    lane = lax.broadcasted_iota(I32, logits.shape, 1).astype(F32)
    cur = logits
    vals, idxs = [], []
    for _ in range(TOP_K):
        mx = jnp.max(cur, axis=-1, keepdims=True)
        am = jnp.min(jnp.where(cur == mx, lane, float(LANES)), axis=-1, keepdims=True)
        vals.append(mx)
        idxs.append(am)
        cur = jnp.where(lane == am, -jnp.inf, cur)
    es = [jnp.exp(v - vals[0]) for v in vals]
    tot = es[0] + es[1] + es[2] + es[3]
    idx_o = jnp.zeros(logits.shape, F32)
    gate_o = jnp.zeros(logits.shape, F32)
    for k in range(TOP_K):
        idx_o = jnp.where(lane == float(k), idxs[k], idx_o)
        gate_o = jnp.where(lane == float(k), es[k] / tot, gate_o)
    idx_ref[...] = idx_o.astype(I32)
    gate_ref[...] = gate_o


def _router(x, mod4, g, wr, br, m_ctx, l_lat):
    m, d = x.shape
    tm = _tile(math.gcd(m_ctx, l_lat), 256)
    wr_p = jnp.zeros((d, LANES), F32).at[:, :N_EXPERTS].set(wr)
    br_p = jnp.full((1, LANES), NEG_BIG, F32).at[0, :N_EXPERTS].set(br)
    return pl.pallas_call(
        _router_kernel, name="router",
        out_shape=(jax.ShapeDtypeStruct((m, d // 2 // LANES, LANES), jnp.uint32),
                   jax.ShapeDtypeStruct((m, LANES), I32),
                   jax.ShapeDtypeStruct((m, LANES), F32)),
        grid=(m // tm,),
        in_specs=[pl.BlockSpec((tm, d), lambda i: (i, 0)),
                  _mod_spec(d, 3, tm, m_ctx, l_lat, 1),
                  _mod_spec(d, 4, tm, m_ctx, l_lat, 1),
                  pl.BlockSpec((1, d), lambda i: (0, 0)),
                  pl.BlockSpec((d, LANES), lambda i: (0, 0)),
                  pl.BlockSpec((1, LANES), lambda i: (0, 0))],
        out_specs=(pl.BlockSpec((tm, d // 2 // LANES, LANES), lambda i: (i, 0, 0)),
                   pl.BlockSpec((tm, LANES), lambda i: (i, 0)),
                   pl.BlockSpec((tm, LANES), lambda i: (i, 0))),
        compiler_params=_cparams(1),
    )(x, mod4, mod4, g.reshape(1, d), wr_p, br_p)


def _route_plan(top_idx, tm, n_tiles):
    m = top_idx.shape[0]
    flat_e = top_idx.reshape(-1)
    onehot = (flat_e[:, None] == jnp.arange(N_EXPERTS, dtype=I32)[None, :]).astype(I32)
    csum = jnp.cumsum(onehot, axis=0)
    rank = jnp.take_along_axis(csum, flat_e[:, None], axis=1)[:, 0] - 1
    counts = csum[-1]
    tiles_e = (counts + tm - 1) // tm
    tile_end = jnp.cumsum(tiles_e)
    tile_start = tile_end - tiles_e
    dest = tile_start[flat_e] * tm + rank
    row_tok = jnp.zeros((n_tiles * tm,), I32).at[dest].set(jnp.arange(m * TOP_K, dtype=I32) // TOP_K)
    return dest.astype(I32), row_tok, tiles_e.astype(I32), tile_start.astype(I32), tile_end.astype(I32)


def _schedule(tiles_e, tile_start, tile_end, n_tiles, nj):
    n_used = tile_end[-1]
    step = jnp.arange(n_tiles * nj, dtype=I32)
    s = jnp.minimum(step, n_used * nj - 1)
    e = jnp.minimum(jnp.sum((s[:, None] >= (tile_end * nj)[None, :]).astype(I32), axis=1), N_EXPERTS - 1)
    local = s - tile_start[e] * nj
    t_e = jnp.maximum(tiles_e[e], 1)
    j = (local // t_e).astype(I32)
    r = local - j * t_e
    tile = (tile_start[e] + r).astype(I32)
    valid = step < n_used * nj
    spare = step - n_used * nj
    out_tile = jnp.where(valid, tile, n_used + spare // nj).astype(I32)
    out_j = jnp.where(valid, j, spare % nj).astype(I32)
    flags = jnp.where(valid, 1 + 2 * (r == 0).astype(I32), 4).astype(I32)
    return e, j, tile, out_tile, out_j, flags


def _gather_kernel(tok_ref, nused_ref, h_hbm, o_ref, buf, sem):
    i = pl.program_id(0)
    tg = o_ref.shape[0]
    n_c = h_hbm.shape[1]

    @pl.when(i < nused_ref[0])
    def _():
        def row_copy(t, r):
            return pltpu.make_async_copy(h_hbm.at[t], buf.at[pl.ds(pl.multiple_of(r * n_c, n_c), n_c), :], sem)

        def issue(r, c):
            row_copy(tok_ref[i * tg + r], r).start()
            return c

        lax.fori_loop(0, tg, issue, 0, unroll=8)

        def wait(r, c):
            row_copy(0, r).wait()
            return c

        lax.fori_loop(0, tg, wait, 0, unroll=8)
        half = n_c * LANES
        for c in range(n_c):
            w = buf[pl.ds(c, tg, stride=n_c), :]
            o_ref[:, c * LANES:(c + 1) * LANES] = pltpu.bitcast(lax.shift_left(w, jnp.uint32(16)), F32).astype(BF16)
            o_ref[:, half + c * LANES:half + (c + 1) * LANES] = pltpu.bitcast(
                w & jnp.uint32(0xFFFF0000), F32).astype(BF16)

    @pl.when(i >= nused_ref[0])
    def _():
        o_ref[...] = jnp.zeros(o_ref.shape, o_ref.dtype)


def _dispatch(h, row_tok, n_used_rows_tiles, tg):
    d = 2 * h.shape[1] * h.shape[2]
    n_rows = row_tok.shape[0]
    grid_spec = pltpu.PrefetchScalarGridSpec(
        num_scalar_prefetch=2, grid=(n_rows // tg,),
        in_specs=[pl.BlockSpec(memory_space=pl.ANY)],
        out_specs=pl.BlockSpec((tg, d), lambda i, tok, nu: (i, 0)),
        scratch_shapes=[pltpu.VMEM((tg * h.shape[1], h.shape[2]), jnp.uint32), pltpu.SemaphoreType.DMA(())])
    return pl.pallas_call(
        _gather_kernel, name="dispatch", grid_spec=grid_spec,
        out_shape=jax.ShapeDtypeStruct((n_rows, d), BF16),
        compiler_params=_cparams(1),
    )(row_tok, n_used_rows_tiles, h)


def _moe_up_kernel(se, sj, st, so, soj, sf, x_ref, wg_ref, wu_ref, bg_ref, bu_ref, o_ref, wgb, wub):
    flag = sf[pl.program_id(0)]

    @pl.when((flag & 4) != 0)
    def _():
        o_ref[...] = jnp.zeros(o_ref.shape, o_ref.dtype)

    @pl.when((flag & 2) != 0)
    def _():
        wgb[...] = wg_ref[...].astype(BF16)
        wub[...] = wu_ref[...].astype(BF16)

    @pl.when((flag & 1) != 0)
    def _():
        x = x_ref[...]
        gate = jnp.dot(x, wgb[...], preferred_element_type=F32) + bg_ref[...]
        up = jnp.dot(x, wub[...], preferred_element_type=F32) + bu_ref[...]
        gate = jnp.minimum(gate, SWIGLU_LIMIT)
        up = jnp.clip(up, -SWIGLU_LIMIT, SWIGLU_LIMIT)
        o_ref[...] = ((up + 1.0) * gate * _sigmoid(SWIGLU_ALPHA * gate)).astype(BF16)


def _moe_up(xs, wg, wu, bg, bu, layer, sched, tm, tn):
    n_rows, d = xs.shape
    depth, e, _, f = wg.shape
    n_steps = sched[0].shape[0]
    w_spec = pl.BlockSpec((None, None, d, tn), lambda s, se, sj, st, so, soj, sf: (layer, se[s], 0, sj[s]))
    b_spec = pl.BlockSpec((None, None, 1, tn), lambda s, se, sj, st, so, soj, sf: (layer, se[s], 0, sj[s]))
    grid_spec = pltpu.PrefetchScalarGridSpec(
        num_scalar_prefetch=6, grid=(n_steps,),
        in_specs=[pl.BlockSpec((tm, d), lambda s, se, sj, st, so, soj, sf: (st[s], 0)),
                  w_spec, w_spec, b_spec, b_spec],
        out_specs=pl.BlockSpec((tm, tn), lambda s, se, sj, st, so, soj, sf: (so[s], soj[s])),
        scratch_shapes=[pltpu.VMEM((d, tn), BF16), pltpu.VMEM((d, tn), BF16)])
    return pl.pallas_call(
        _moe_up_kernel, name="moe_up", grid_spec=grid_spec,
        out_shape=jax.ShapeDtypeStruct((n_rows, f), BF16),
        compiler_params=_cparams(1),
    )(*sched, xs, wg, wu, bg.reshape(depth, e, 1, f), bu.reshape(depth, e, 1, f))


def _moe_down_kernel(se, sj, st, so, soj, sf, h_ref, wd_ref, bd_ref, o_ref, wdb):
    flag = sf[pl.program_id(0)]

    @pl.when((flag & 4) != 0)
    def _():
        o_ref[...] = jnp.zeros(o_ref.shape, o_ref.dtype)

    @pl.when((flag & 2) != 0)
    def _():
        wdb[...] = wd_ref[...].astype(BF16)

    @pl.when((flag & 1) != 0)
    def _():
        o_ref[...] = jnp.dot(h_ref[...], wdb[...], preferred_element_type=F32) + bd_ref[...]


def _moe_down(hid, wd, bd, layer, sched, tm, tn):
    n_rows, f = hid.shape
    depth, e, _, d = wd.shape
    n_steps = sched[0].shape[0]
    grid_spec = pltpu.PrefetchScalarGridSpec(
        num_scalar_prefetch=6, grid=(n_steps,),
        in_specs=[pl.BlockSpec((tm, f), lambda s, se, sj, st, so, soj, sf: (st[s], 0)),
                  pl.BlockSpec((None, None, f, tn), lambda s, se, sj, st, so, soj, sf: (layer, se[s], 0, sj[s])),
                  pl.BlockSpec((None, None, 1, tn), lambda s, se, sj, st, so, soj, sf: (layer, se[s], 0, sj[s]))],
        out_specs=pl.BlockSpec((tm, tn), lambda s, se, sj, st, so, soj, sf: (so[s], soj[s])),
        scratch_shapes=[pltpu.VMEM((f, tn), BF16)])
    return pl.pallas_call(
        _moe_down_kernel, name="moe_down", grid_spec=grid_spec,
        out_shape=jax.ShapeDtypeStruct((n_rows, d), F32),
        compiler_params=_cparams(1),
    )(*sched, hid, wd, bd.reshape(depth, e, 1, d))


def _combine_kernel(pos_ref, y_hbm, x_ref, gate_ref, g2_ref, nf_ref, o_ref, buf, sem, *, final_norm):
    i = pl.program_id(0)
    tc = x_ref.shape[0]

    def issue(t, c):
        for k in range(TOP_K):
            p = pos_ref[(i * tc + t) * TOP_K + k]
            pltpu.make_async_copy(y_hbm.at[pl.ds(p, 1)], buf.at[k, pl.ds(t, 1)], sem).start()
        return c

    lax.fori_loop(0, tc, issue, 0)

    def wait(t, c):
        for k in range(TOP_K):
            pltpu.make_async_copy(y_hbm.at[pl.ds(0, 1)], buf.at[k, pl.ds(t, 1)], sem).wait()
        return c

    lax.fori_loop(0, tc, wait, 0)
    gates = gate_ref[...]
    acc = gates[:, 0:1] * buf[0]
    for k in range(1, TOP_K):
        acc += gates[:, k:k + 1] * buf[k]
    out = x_ref[...] + g2_ref[...] * acc
    if final_norm:
        out = _rms(out, nf_ref[...])
    o_ref[...] = out


def _combine(y, pos, x, gates, mod4, norm_f, m_ctx, l_lat, final_norm):
    m, d = x.shape
    tc = _tile(math.gcd(m_ctx, l_lat), 128)
    grid_spec = pltpu.PrefetchScalarGridSpec(
        num_scalar_prefetch=1, grid=(m // tc,),
        in_specs=[pl.BlockSpec(memory_space=pl.ANY),
                  pl.BlockSpec((tc, d), lambda i, pos: (i, 0)),
                  pl.BlockSpec((tc, LANES), lambda i, pos: (i, 0)),
                  pl.BlockSpec((None, None, 1, d), lambda i, pos: (_mod_row(i, tc, m_ctx, l_lat), 5, 0, 0)),
                  pl.BlockSpec((1, d), lambda i, pos: (0, 0))],
        out_specs=pl.BlockSpec((tc, d), lambda i, pos: (i, 0)),
        scratch_shapes=[pltpu.VMEM((TOP_K, tc, d), F32), pltpu.SemaphoreType.DMA(())])
    return pl.pallas_call(
        functools.partial(_combine_kernel, final_norm=final_norm), name="combine", grid_spec=grid_spec,
        out_shape=jax.ShapeDtypeStruct((m, d), F32),
        compiler_params=_cparams(1),
    )(pos, y, x, gates, mod4, norm_f.reshape(1, d))


def _moe(x, mod4, norm2_g, wr, br, wg, bg, wu, bu, wd, bd, layer, norm_f, m_ctx, l_lat, final_norm):
    m, d = x.shape
    f = wg.shape[3]
    tm = _tile(m, 512)
    n_tiles = -(-(m * TOP_K + N_EXPERTS * (tm - 1)) // tm)
    h, top_idx, gates = _router(x, mod4, norm2_g, wr, br, m_ctx, l_lat)
    dest, row_tok, tiles_e, tile_start, tile_end = _route_plan(top_idx[:, :TOP_K], tm, n_tiles)
    xs = _dispatch(h, row_tok, tile_end[-1:], tm)
    tn_up = _tile(f, 512)
    sched_up = _schedule(tiles_e, tile_start, tile_end, n_tiles, f // tn_up)
    hid = _moe_up(xs, wg, wu, bg, bu, layer, sched_up, tm, tn_up)
    tn_dn = _tile(d, 1024)
    sched_dn = _schedule(tiles_e, tile_start, tile_end, n_tiles, d // tn_dn)
    y = _moe_down(hid, wd, bd, layer, sched_dn, tm, tn_dn)
    return _combine(y, dest, x, gates, mod4, norm_f, m_ctx, l_lat, final_norm)


def _rope_tables(l, d):
    t = jnp.arange(l, dtype=I32)
    row = (t // GRID_W).astype(F32)[:, None]
    col = (t % GRID_W).astype(F32)[:, None]
    q = d // 4
    lane = jnp.arange(LANES, dtype=I32) % d
    pos = jnp.where((lane // (2 * q))[None, :] == 0, row, col)
    inv = ROPE_THETA ** (-(lane % q).astype(F32) / q)
    ang = pos * inv[None, :]
    sign = jnp.where((lane % (2 * q)) < q, -1.0, 1.0)[None, :]
    return jnp.cos(ang), jnp.sin(ang) * sign


def _rope(x, cos, sin, d):
    q = d // 4
    lane = lax.broadcasted_iota(I32, x.shape, 1)
    fwd = pltpu.roll(x, LANES - q, 1)
    bwd = pltpu.roll(x, q, 1)
    return x * cos + jnp.where((lane % (2 * q)) < q, fwd, bwd) * sin


def _attend(q_scr, k_scr, v_scr, o_ref, col0, tq):
    dv = v_scr.shape[1]

    def body(i, c):
        r0 = pl.multiple_of(i * tq, tq)
        s = lax.dot_general(q_scr[pl.ds(r0, tq), :], k_scr[...], (((1,), (1,)), ((), ())),
                            preferred_element_type=F32)
        e = jnp.exp(s - jnp.max(s, axis=-1, keepdims=True))
        o = jnp.dot(e.astype(BF16), v_scr[...], preferred_element_type=F32)
        o = o * (1.0 / jnp.sum(e, axis=-1, keepdims=True))
        o_ref[pl.ds(r0, tq), col0:col0 + dv] = o.astype(o_ref.dtype)
        return c

    lax.fori_loop(0, q_scr.shape[0] // tq, body, 0)


def _gqa_kernel(*refs, has_cache, grp):
    if has_cache:
        q_ref, k_ref, v_ref, qn_ref, kn_ref, ck_ref, cv_ref, cos_ref, sin_ref, o_ref, q_scr, k_scr, v_scr = refs
    else:
        q_ref, k_ref, v_ref, qn_ref, kn_ref, o_ref, kout_ref, q_scr, k_scr, v_scr = refs
    l = k_ref.shape[0]
    k = _rms(k_ref[...], kn_ref[...])
    if has_cache:
        p = ck_ref.shape[0]
        k_scr[0:p, :] = ck_ref[...].astype(BF16)
        v_scr[0:p, :] = cv_ref[...].astype(BF16)
        k_scr[p:p + l, :] = _rope(k, cos_ref[...], sin_ref[...], GQA_HD).astype(BF16)
        v_scr[p:p + l, :] = v_ref[...].astype(BF16)
    else:
        kout_ref[...] = k
        k_scr[...] = k.astype(BF16)
        v_scr[...] = v_ref[...].astype(BF16)
    scale = GQA_HD ** -0.5
    for g in range(grp):
        q = _rms(q_ref[:, g * GQA_HD:(g + 1) * GQA_HD], qn_ref[...])
        if has_cache:
            q = _rope(q, cos_ref[...], sin_ref[...], GQA_HD)
        q_scr[...] = (q * scale).astype(BF16)
        _attend(q_scr, k_scr, v_scr, o_ref, g * GQA_HD, _tile(l, 256))


def _gqa(xin, cols, qn, kn, b, l, row_blk0, cache=None, layer=0):
    q0, k0, v0 = cols
    n_q = k0 - q0
    grp = n_q // GQA_KV // GQA_HD
    qw = grp * GQA_HD
    has_cache = cache is not None
    in_specs = [pl.BlockSpec((l, qw), lambda i, h: (row_blk0 + i, q0 // qw + h)),
                pl.BlockSpec((l, GQA_HD), lambda i, h: (row_blk0 + i, k0 // GQA_HD + h)),
                pl.BlockSpec((l, GQA_HD), lambda i, h: (row_blk0 + i, v0 // GQA_HD + h)),
                pl.BlockSpec((1, GQA_HD), lambda i, h: (0, 0)),
                pl.BlockSpec((1, GQA_HD), lambda i, h: (0, 0))]
    args = [xin, xin, xin, qn.reshape(1, GQA_HD), kn.reshape(1, GQA_HD)]
    lk = l
    o_shape = jax.ShapeDtypeStruct((b * l, n_q), BF16)
    o_spec = pl.BlockSpec((l, qw), lambda i, h: (i, h))
    if has_cache:
        ck, cv = cache
        p = ck.shape[2]
        lk = l + p
        cspec = pl.BlockSpec((None, None, p, GQA_HD), lambda i, h: (i, layer, 0, h))
        tspec = pl.BlockSpec((l, LANES), lambda i, h: (0, 0))
        cos, sin = _rope_tables(l, GQA_HD)
        in_specs += [cspec, cspec, tspec, tspec]
        args += [ck, cv, cos, sin]
        out_shape, out_specs = o_shape, o_spec
    else:
        out_shape = (o_shape, jax.ShapeDtypeStruct((b * l, GQA_KV * GQA_HD), F32))
        out_specs = (o_spec, pl.BlockSpec((l, GQA_HD), lambda i, h: (i, h)))
    return pl.pallas_call(
        functools.partial(_gqa_kernel, has_cache=has_cache, grp=grp), name="gqa_lat" if has_cache else "gqa_ctx",
        out_shape=out_shape, grid=(b, GQA_KV), in_specs=in_specs, out_specs=out_specs,
        scratch_shapes=[pltpu.VMEM((l, GQA_HD), BF16), pltpu.VMEM((lk, GQA_HD), BF16), pltpu.VMEM((lk, GQA_HD), BF16)],
        compiler_params=_cparams(2),
    )(*args)


def _mla_kernel(*refs, has_cache, heads):
    if has_cache:
        (qc_ref, ckv_ref, kr_ref, qn_ref, kvn_ref, wqn_ref, wqr_ref, wuk_ref, wuv_ref, cc_ref, ckr_ref, cos_ref,
         sin_ref, o_ref, qn_scr, qr_scr, kn_scr, vv_scr, kr_scr) = refs
    else:
        (qc_ref, ckv_ref, kr_ref, qn_ref, kvn_ref, wqn_ref, wqr_ref, wuk_ref, wuv_ref, o_ref, ckv_out_ref,
         qn_scr, qr_scr, kn_scr, vv_scr, kr_scr) = refs
    l = qc_ref.shape[0]
    scale = (MLA_NOPE + MLA_ROPE) ** -0.5
    qc = _rms(qc_ref[...], qn_ref[...]).astype(BF16)
    qn_scr[...] = jnp.dot(qc, wqn_ref[...], preferred_element_type=F32) * scale
    qr = jnp.dot(qc, wqr_ref[...], preferred_element_type=F32) * scale
    ckv = _rms(ckv_ref[...], kvn_ref[...])
    kr = kr_ref[...]
    lane = lax.broadcasted_iota(I32, kr.shape, 1)
    kr = jnp.where(lane < MLA_ROPE, kr, pltpu.roll(kr, MLA_ROPE, 1))
    if has_cache:
        p = cc_ref.shape[0]
        cos, sin = cos_ref[...], sin_ref[...]
        for c in range(qr.shape[1] // LANES):
            qr_scr[:, c * LANES:(c + 1) * LANES] = _rope(qr[:, c * LANES:(c + 1) * LANES], cos, sin, MLA_ROPE)
        kr_scr[0:p, :] = ckr_ref[...].astype(BF16)
        kr_scr[p:p + l, :] = _rope(kr, cos, sin, MLA_ROPE).astype(BF16)
        cc = cc_ref[...].astype(BF16)
        kn_scr[0:p, :] = jnp.dot(cc, wuk_ref[...], preferred_element_type=F32).astype(BF16)
        vv_scr[0:p, :] = jnp.dot(cc, wuv_ref[...], preferred_element_type=F32).astype(BF16)
    else:
        p = 0
        qr_scr[...] = qr
        kr_scr[...] = kr.astype(BF16)
        ckv_out_ref[...] = ckv
    cb = ckv.astype(BF16)
    kn_scr[p:p + l, :] = jnp.dot(cb, wuk_ref[...], preferred_element_type=F32).astype(BF16)
    vv_scr[p:p + l, :] = jnp.dot(cb, wuv_ref[...], preferred_element_type=F32).astype(BF16)
    tq = _tile(l, 256)
    nt = (((1,), (1,)), ((), ()))
    for h in range(heads):
        c = (h * MLA_ROPE) // LANES

        def tile(t, carry, h=h, c=c):
            rows = slice(0, tq) if l == tq else pl.ds(pl.multiple_of(t * tq, tq), tq)
            qrh = qr_scr[rows, c * LANES:(c + 1) * LANES]
            lq = lax.broadcasted_iota(I32, qrh.shape, 1)
            own = (lq // MLA_ROPE) == (h % (LANES // MLA_ROPE))
            s = lax.dot_general(qn_scr[rows, h * MLA_NOPE:(h + 1) * MLA_NOPE].astype(BF16),
                                kn_scr[:, h * MLA_NOPE:(h + 1) * MLA_NOPE], nt, preferred_element_type=F32)
            s += lax.dot_general(jnp.where(own, qrh, 0.0).astype(BF16), kr_scr[...], nt, preferred_element_type=F32)
            e = jnp.exp(s - jnp.max(s, axis=-1, keepdims=True))
            o = jnp.dot(e.astype(BF16), vv_scr[:, h * MLA_V:(h + 1) * MLA_V], preferred_element_type=F32)
            o = o * (1.0 / jnp.sum(e, axis=-1, keepdims=True))
            o_ref[rows, h * MLA_V:(h + 1) * MLA_V] = o.astype(o_ref.dtype)
            return carry

        if l == tq:
            tile(0, 0)
        else:
            lax.fori_loop(0, l // tq, tile, 0)


def _mla(xin, col0, qn, kvn, w_uq, w_ukv, b, l, row_blk0, cache=None, layer=0):
    heads = w_uq.shape[1] // (MLA_NOPE + MLA_ROPE)
    wq = w_uq.reshape(MLA_Q_LORA, heads, MLA_NOPE + MLA_ROPE)
    wqn = wq[:, :, :MLA_NOPE].reshape(MLA_Q_LORA, heads * MLA_NOPE).astype(BF16)
    wqr = wq[:, :, MLA_NOPE:].reshape(MLA_Q_LORA, heads * MLA_ROPE).astype(BF16)
    wkv = w_ukv.reshape(MLA_KV_LORA, heads, MLA_NOPE + MLA_V)
    wuk = wkv[:, :, :MLA_NOPE].reshape(MLA_KV_LORA, heads * MLA_NOPE).astype(BF16)
    wuv = wkv[:, :, MLA_NOPE:].reshape(MLA_KV_LORA, heads * MLA_V).astype(BF16)
    has_cache = cache is not None
    c_kv = col0 + MLA_Q_LORA
    c_kr = c_kv + MLA_KV_LORA
    full = lambda a: pl.BlockSpec(a.shape, lambda i: (0,) * a.ndim)
    in_specs = [pl.BlockSpec((l, MLA_Q_LORA), lambda i: (row_blk0 + i, col0 // MLA_Q_LORA)),
                pl.BlockSpec((l, MLA_KV_LORA), lambda i: (row_blk0 + i, c_kv // MLA_KV_LORA)),
                pl.BlockSpec((l, LANES), lambda i: (row_blk0 + i, c_kr // LANES)),
                pl.BlockSpec((1, MLA_Q_LORA), lambda i: (0, 0)),
                pl.BlockSpec((1, MLA_KV_LORA), lambda i: (0, 0)),
                full(wqn), full(wqr), full(wuk), full(wuv)]
    args = [xin, xin, xin, qn.reshape(1, -1), kvn.reshape(1, -1), wqn, wqr, wuk, wuv]
    o_shape = jax.ShapeDtypeStruct((b * l, heads * MLA_V), BF16)
    o_spec = pl.BlockSpec((l, heads * MLA_V), lambda i: (i, 0))
    lk = l
    if has_cache:
        cc, ckr2 = cache
        p = cc.shape[2]
        lk = l + p
        cos, sin = _rope_tables(l, MLA_ROPE)
        in_specs += [pl.BlockSpec((None, None, p, MLA_KV_LORA), lambda i: (i, layer, 0, 0)),
                     pl.BlockSpec((None, None, p, LANES), lambda i: (i, layer, 0, 0)),
                     pl.BlockSpec((l, LANES), lambda i: (0, 0)), pl.BlockSpec((l, LANES), lambda i: (0, 0))]
        args += [cc, ckr2, cos, sin]
        out_shape, out_specs = o_shape, o_spec
    else:
        out_shape = (o_shape, jax.ShapeDtypeStruct((b * l, MLA_KV_LORA), F32))
        out_specs = (o_spec, pl.BlockSpec((l, MLA_KV_LORA), lambda i: (i, 0)))
    scratch = [pltpu.VMEM((l, heads * MLA_NOPE), F32), pltpu.VMEM((l, heads * MLA_ROPE), F32),
               pltpu.VMEM((lk, heads * MLA_NOPE), BF16), pltpu.VMEM((lk, heads * MLA_V), BF16),
               pltpu.VMEM((lk, LANES), BF16)]
    return pl.pallas_call(
        functools.partial(_mla_kernel, has_cache=has_cache, heads=heads), name="mla_lat" if has_cache else "mla_ctx",
        out_shape=out_shape, grid=(b,), in_specs=in_specs, out_specs=out_specs, scratch_shapes=scratch,
        compiler_params=_cparams(1),
    )(*args)


def _cmul(ar, ai, br, bi):
    return ar * br - ai * bi, ar * bi + ai * br


def _s5_operators(lam_re, lam_im, log_dt, b_re, b_im, c_re, c_im, d_skip):
    g, p = lam_re.shape[1:]
    t = SSM_T
    n = jnp.arange(t + 1, dtype=F32)[:, None, None]
    us, ws, ks, a_rows = [], [], [], []
    for d in range(2):
        lr, li = lam_re[d], lam_im[d]
        dt = jnp.exp(log_dt[d])[:, None]
        pw_re = jnp.exp(n * (lr * dt)[None]) * jnp.cos(n * (li * dt)[None])
        pw_im = jnp.exp(n * (lr * dt)[None]) * jnp.sin(n * (li * dt)[None])
        den = lr * lr + li * li
        cf_re, cf_im = _cmul(pw_re[1] - 1.0, pw_im[1], lr / den, -li / den)
        bb_re, bb_im = _cmul(cf_re[..., None], cf_im[..., None], b_re[d], b_im[d])
        cl_re, cl_im = _cmul(c_re[None], c_im[None], pw_re[:t, :, None, :], pw_im[:t, :, None, :])
        ks.append(jnp.einsum("ngcp,gpe->ngce", cl_re, bb_re, precision=HIGHEST)
                  - jnp.einsum("ngcp,gpe->ngce", cl_im, bb_im, precision=HIGHEST))
        e = jnp.arange(t)[::-1] if d == 0 else jnp.arange(t)
        u_re, u_im = _cmul(pw_re[e][:, :, None, :], pw_im[e][:, :, None, :],
                           jnp.swapaxes(bb_re, 1, 2)[None], jnp.swapaxes(bb_im, 1, 2)[None])
        us += [jnp.moveaxis(u_re, 1, 0).reshape(g, t * SSM_CH, p), jnp.moveaxis(u_im, 1, 0).reshape(g, t * SSM_CH, p)]
        e = jnp.arange(1, t + 1) if d == 0 else jnp.arange(t, 0, -1)
        w_re, w_im = _cmul(c_re[None], c_im[None], pw_re[e][:, :, None, :], pw_im[e][:, :, None, :])
        to_w = lambda a: jnp.transpose(a, (1, 3, 0, 2)).reshape(g, p, t * SSM_CH)
        ws += [to_w(w_re), -to_w(w_im)]
        a_rows += [jnp.concatenate([pw_re[t], pw_re[t]], -1), jnp.concatenate([-pw_im[t], pw_im[t]], -1)]
    s_in = jnp.arange(t)[:, None]
    s_out = jnp.arange(t)[None, :]
    kf = jnp.where((s_out >= s_in)[..., None, None, None], ks[0][jnp.clip(s_out - s_in, 0, t - 1)], 0.0)
    kb = jnp.where((s_in >= s_out)[..., None, None, None], ks[1][jnp.clip(s_in - s_out, 0, t - 1)], 0.0)
    skip = (s_in == s_out)[..., None, None, None] * (d_skip.reshape(g, SSM_CH)[None, None, :, :, None]
                                                     * jnp.eye(SSM_CH, dtype=F32)[None, None, None])
    tg = jnp.transpose(kf + kb + skip, (2, 0, 4, 1, 3)).reshape(g, t * SSM_CH, t * SSM_CH)
    u_all = jnp.concatenate(us, axis=-1)
    w_all = jnp.concatenate([tg] + ws, axis=1)
    a_all = jnp.stack(a_rows + a_rows, axis=1)
    return u_all, w_all, a_all


def _gelu_tanh(x):
    return 0.5 * x * (1.0 + jnp.tanh(0.7978845608028654 * (x + 0.044715 * (x * x * x))))


def _s5_kernel(x_ref, u_ref, w_ref, a_ref, h0_ref, y_ref, fin_ref, v_scr, hp_scr, gn_scr, *, nb):
    rows = x_ref.shape[0]
    nj = rows // nb
    x = x_ref[...]
    v_scr[...] = jnp.dot(x, u_ref[...], precision=HIGHEST, preferred_element_type=F32)
    two_p = a_ref.shape[1]
    a = a_ref[...]

    def step(j, carry):
        h, g = carry
        rf = pl.multiple_of(j * nb, nb)
        rb = pl.multiple_of((nj - 1 - j) * nb, nb)
        hp_scr[pl.ds(rf, nb), :] = h
        gn_scr[pl.ds(rb, nb), :] = g
        h = a[0:1] * h + a[1:2] * pltpu.roll(h, two_p // 2, 1) + v_scr[pl.ds(rf, nb), 0:two_p]
        g = a[2:3] * g + a[3:4] * pltpu.roll(g, two_p // 2, 1) + v_scr[pl.ds(rb, nb), two_p:2 * two_p]
        return h, g

    h_fin, g_fin = lax.fori_loop(0, nj, step, (h0_ref[0], h0_ref[1]), unroll=2)
    fin_ref[0] = h_fin
    fin_ref[1] = g_fin
    tc = x.shape[1]
    y = jnp.dot(x, w_ref[0:tc, :], precision=HIGHEST, preferred_element_type=F32)
    y += jnp.dot(hp_scr[...], w_ref[tc:tc + two_p, :], precision=HIGHEST, preferred_element_type=F32)
    y += jnp.dot(gn_scr[...], w_ref[tc + two_p:tc + 2 * two_p, :], precision=HIGHEST, preferred_element_type=F32)
    y_ref[...] = _gelu_tanh(y)


def _s5_scan(xg, u_all, w_all, a_all, h0, nb):
    g, rows, tc = xg.shape
    two_p = a_all.shape[2]
    blk = lambda a: pl.BlockSpec((None,) + a.shape[1:], lambda i: (i,) + (0,) * (a.ndim - 1))
    return pl.pallas_call(
        functools.partial(_s5_kernel, nb=nb), name="s5",
        out_shape=(jax.ShapeDtypeStruct((g, rows, tc), F32), jax.ShapeDtypeStruct((g, 2, nb, two_p), F32)),
        grid=(g,),
        in_specs=[blk(xg), blk(u_all), blk(w_all), blk(a_all), blk(h0)],
        out_specs=(pl.BlockSpec((None, rows, tc), lambda i: (i, 0, 0)),
                   pl.BlockSpec((None, 2, nb, two_p), lambda i: (i, 0, 0, 0))),
        scratch_shapes=[pltpu.VMEM((rows, 2 * two_p), F32), pltpu.VMEM((rows, two_p), F32),
                        pltpu.VMEM((rows, two_p), F32)],
        compiler_params=_cparams(1),
    )(xg, u_all, w_all, a_all, h0)


def _glu_kernel(y_ref, w_ref, b_ref, o_ref):
    y = y_ref[...]
    z = jnp.dot(y.astype(BF16), w_ref[...], preferred_element_type=F32) + b_ref[...]
    o_ref[...] = (y * _sigmoid(z)).astype(o_ref.dtype)


def _glu(y, w, b):
    m, n = y.shape
    tm = _tile(m, 512)
    return pl.pallas_call(
        _glu_kernel, name="s5_glu",
        out_shape=jax.ShapeDtypeStruct((m, n), BF16),
        grid=(m // tm,),
        in_specs=[pl.BlockSpec((tm, n), lambda i: (i, 0)), pl.BlockSpec((n, n), lambda i: (0, 0)),
                  pl.BlockSpec((1, n), lambda i: (0, 0))],
        out_specs=pl.BlockSpec((tm, n), lambda i: (i, 0)),
        compiler_params=_cparams(1),
    )(y, w.astype(BF16), b.reshape(1, n))


def _s5_path(u, ops, b, l, h0):
    g = ops[0].shape[0]
    nj = l // SSM_T
    xg = u.reshape(b, nj, SSM_T, g, SSM_CH).transpose(3, 1, 0, 2, 4).reshape(g, nj * b, SSM_T * SSM_CH)
    yg, fin = _s5_scan(xg, *ops, h0, b)
    y = yg.reshape(g, nj, b, SSM_T, SSM_CH).transpose(2, 1, 3, 0, 4).reshape(b * l, g * SSM_CH)
    return y, fin


def _softplus(x):
    return jnp.maximum(x, 0.0) + jnp.log1p(jnp.exp(-jnp.abs(x)))


def _dn_conv(x_ref, w_ref):
    x = x_ref[...]
    l = x.shape[0]
    w = w_ref[...]
    t = lax.broadcasted_iota(I32, x.shape, 0)
    pad = (DN_CONV - 1) // 2
    acc = x * w[pad:pad + 1]
    for tau in range(DN_CONV):
        delta = tau - pad
        if delta == 0:
            continue
        shifted = pltpu.roll(x, (-delta) % l, 0)
        ok = (t + delta >= 0) & (t + delta < l)
        acc += jnp.where(ok, shifted, 0.0) * w[tau:tau + 1]
    return _silu(acc)


def _l2(x):
    return x * lax.rsqrt(jnp.sum(x * x, axis=-1, keepdims=True) + EPS)


def _dn_intra(ci, q_scr, k_scr, v_scr, gb_scr, u_scr, w_scr, qd_scr, kd_scr, qk_scr, et_scr):
    c = DN_CHUNK
    hd = DN_HEAD_DIM
    ri = lax.broadcasted_iota(I32, (c, c), 0)
    cj = lax.broadcasted_iota(I32, (c, c), 1)
    eye = jnp.where(ri == cj, 1.0, 0.0)
    incl = [(cj <= ri), (cj >= ri)]
    strict = [(cj < ri), (cj > ri)]
    rows, one = [], []
    for i in ci:
        static = isinstance(i, int)
        rows.append(slice(i * c, (i + 1) * c) if static else pl.ds(pl.multiple_of(i * c, c), c))
        one.append(slice(i, i + 1) if static else pl.ds(i, 1))
    n = len(ci)
    pairs = [(t, d) for t in range(n) for d in range(2)]
    q = [q_scr[r, :] for r in rows]
    k = [k_scr[r, :] for r in rows]
    kk = [_bdot_nt(k[t], k[t]) for t in range(n)]
    qk_raw = [_bdot_nt(q[t], k[t]) for t in range(n)]
    g = [gb_scr[rows[t], 2 * d * hd:(2 * d + 1) * hd] for t, d in pairs]
    beta = [gb_scr[rows[t], (2 * d + 1) * hd:(2 * d + 2) * hd] for t, d in pairs]
    gc = [jnp.dot(incl[d].astype(F32), g[p], precision=HIGHEST, preferred_element_type=F32)
          for p, (t, d) in enumerate(pairs)]
    lane = lax.broadcasted_iota(I32, (c, hd), 1)
    ones = jnp.ones((c, hd), F32)
    gc_row = [lax.dot_general(ones, jnp.where(lane == 0, x, 0.0), (((1,), (1,)), ((), ())),
                              precision=HIGHEST, preferred_element_type=F32) for x in gc]
    dmat = [jnp.exp(jnp.where(incl[d], gc[p][:, 0:c] - gc_row[p], NEG_BIG)) for p, (t, d) in enumerate(pairs)]
    a = [jnp.where(strict[d], beta[p][:, 0:c] * kk[t] * dmat[p], 0.0) for p, (t, d) in enumerate(pairs)]
    inv = [eye - x for x in a]
    pw = [_dot3(x, x) for x in a]
    for r in range(int(math.log2(c)) - 1):
        inv = [x + _dot3(x, y) for x, y in zip(inv, pw)]
        if r + 2 < int(math.log2(c)):
            pw = [_dot3(y, y) for y in pw]
    egc = [jnp.exp(x) for x in gc]
    tot = [gc[p][c - 1:c, :] if d == 0 else gc[p][0:1, :] for p, (t, d) in enumerate(pairs)]
    u = [_dot3(inv[p], v_scr[rows[t], :] * beta[p]) for p, (t, d) in enumerate(pairs)]
    w = [_dot3(inv[p], k[t] * beta[p] * egc[p]) for p, (t, d) in enumerate(pairs)]
    for p, (t, d) in enumerate(pairs):
        u_scr[d, rows[t], :] = u[p]
        w_scr[d, rows[t], :] = w[p]
        qd_scr[d, rows[t], :] = q[t] * egc[p]
        kd_scr[d, rows[t], :] = k[t] * jnp.exp(tot[p] - gc[p])
        qk_scr[d, rows[t], :] = qk_raw[t] * dmat[p]
        et_scr[d, one[t], :] = jnp.exp(tot[p])


def _dn_inter(ci, u_scr, w_scr, qd_scr, kd_scr, qk_scr, et_scr, s_scr, o_scr):
    c = DN_CHUNK
    rows = [pl.ds(pl.multiple_of(i * c, c), c) for i in ci]
    s = [s_scr[d] for d in range(2)]
    ws = [_bdot(jnp.concatenate([w_scr[d, rows[d], :], qd_scr[d, rows[d], :]], axis=0), s[d]) for d in range(2)]
    v_new = [u_scr[d, rows[d], :] - ws[d][0:c] for d in range(2)]
    o = [ws[d][c:2 * c] + _bdot(qk_scr[d, rows[d], :], v_new[d]) for d in range(2)]
    s_new = [s[d] * et_scr[d, pl.ds(ci[d], 1), :] + _bdot_tn(kd_scr[d, rows[d], :], v_new[d]) for d in range(2)]
    for d in range(2):
        o_scr[d, rows[d], :] = o[d]
        s_scr[d] = s_new[d]


def _dn_kernel(*refs, has_state, nh):
    if has_state:
        (q_ref, k_ref, v_ref, z_ref, ab_ref, wq_ref, wk_ref, wv_ref, alog_ref, bias_ref, ng_ref, s0_ref, o_ref,
         q_scr, k_scr, v_scr, gb_scr, s_scr, o_scr, *chunk_scr) = refs
    else:
        (q_ref, k_ref, v_ref, z_ref, ab_ref, wq_ref, wk_ref, wv_ref, alog_ref, bias_ref, ng_ref, o_ref, sfin_ref,
         q_scr, k_scr, v_scr, gb_scr, s_scr, o_scr, *chunk_scr) = refs
    hd = DN_HEAD_DIM
    h = pl.program_id(1)
    l = q_ref.shape[0]
    n = l // DN_CHUNK
    q_scr[...] = _l2(_dn_conv(q_ref, wq_ref)) * hd ** -0.5
    k_scr[...] = _l2(_dn_conv(k_ref, wk_ref))
    v_scr[...] = _dn_conv(v_ref, wv_ref)
    x = ab_ref[...]
    lane = lax.broadcasted_iota(I32, x.shape, 1)
    is_decay = ((lane // nh) % 2) == 0
    mix = jnp.where(is_decay, -jnp.exp(alog_ref[...]) * _softplus(x + bias_ref[...]), _sigmoid(x))
    sr = lax.broadcasted_iota(I32, (LANES, 4 * hd), 0)
    sc = lax.broadcasted_iota(I32, (LANES, 4 * hd), 1)
    sel = (sr == LANES // 2 + nh * (sc // hd) + h).astype(F32)
    gb_scr[...] = jnp.dot(mix, sel, precision=HIGHEST, preferred_element_type=F32)
    if has_state:
        s_scr[...] = s0_ref[...]
    else:
        s_scr[...] = jnp.zeros(s_scr.shape, F32)

    per_body = min(n, 4)

    def intra(i, carry):
        _dn_intra([i * per_body + t for t in range(per_body)], q_scr, k_scr, v_scr, gb_scr, *chunk_scr)
        return carry

    if n == per_body:
        intra(0, 0)
    else:
        lax.fori_loop(0, n // per_body, intra, 0)

    def inter(c, carry):
        _dn_inter([c, n - 1 - c], *chunk_scr, s_scr, o_scr)
        return carry

    lax.fori_loop(0, n, inter, 0)
    o = _rms(o_scr[0] + o_scr[1], ng_ref[...]) * _silu(z_ref[...])
    o_ref[...] = o.astype(o_ref.dtype)
    if not has_state:
        sfin_ref[...] = s_scr[...]


def _deltanet(xin, ab_col, conv_w, a_log, dt_bias, norm_g, b, l, row_blk0, state=None, layer=0):
    hd = DN_HEAD_DIM
    nh = a_log.shape[1]
    has_state = state is not None
    lane_par = lambda p: jnp.zeros((1, LANES), F32).at[0, LANES // 2:LANES // 2 + 4 * nh].set(
        jnp.concatenate([p[0], jnp.zeros_like(p[0]), p[1], jnp.zeros_like(p[1])]))
    xspec = lambda sec: pl.BlockSpec((l, hd), lambda i, h: (row_blk0 + i, sec * nh + h))
    wspec = lambda sec: pl.BlockSpec((DN_CONV, hd), lambda i, h: (0, sec * nh + h))
    vec = pl.BlockSpec((1, LANES), lambda i, h: (0, 0))
    in_specs = [xspec(0), xspec(1), xspec(2), xspec(3),
                pl.BlockSpec((l, LANES), lambda i, h: (row_blk0 + i, ab_col // LANES)),
                wspec(0), wspec(1), wspec(2), vec, vec, vec]
    args = [xin, xin, xin, xin, xin, conv_w, conv_w, conv_w, lane_par(a_log), lane_par(dt_bias),
            norm_g.reshape(1, hd)]
    o_shape = jax.ShapeDtypeStruct((b * l, nh * hd), BF16)
    o_spec = pl.BlockSpec((l, hd), lambda i, h: (i, h))
    if has_state:
        in_specs.append(pl.BlockSpec((None, None, 2, None, hd, hd), lambda i, h: (i, layer, 0, h, 0, 0)))
        args.append(state)
        out_shape, out_specs = o_shape, o_spec
    else:
        out_shape = (o_shape, jax.ShapeDtypeStruct((b, 2, nh, hd, hd), F32))
        out_specs = (o_spec, pl.BlockSpec((None, 2, None, hd, hd), lambda i, h: (i, 0, h, 0, 0)))
    scratch = [pltpu.VMEM((l, hd), F32), pltpu.VMEM((l, hd), F32), pltpu.VMEM((l, hd), F32),
               pltpu.VMEM((l, 4 * hd), F32), pltpu.VMEM((2, hd, hd), F32), pltpu.VMEM((2, l, hd), F32)]
    scratch += [pltpu.VMEM((2, l, hd), F32)] * 4 + [pltpu.VMEM((2, l, DN_CHUNK), F32),
                                                    pltpu.VMEM((2, l // DN_CHUNK, hd), F32)]
    return pl.pallas_call(
        functools.partial(_dn_kernel, has_state=has_state, nh=nh), name="deltanet_lat" if has_state else "deltanet_ctx",
        out_shape=out_shape, grid=(b, nh), in_specs=in_specs, out_specs=out_specs, scratch_shapes=scratch,
        compiler_params=_cparams(2),
    )(*args)


def _round_up(n, k):
    return -(-n // k) * k


def kernel(x_prompt, x_sample, cache_gqa_k, cache_gqa_v, cache_mla_ckv, cache_mla_krope, state_dn, state_ssm_re, state_ssm_im, c, c_ctx, w_ada, b_ada, norm1_g, norm2_g, w_in, dn_conv, dn_a_log, dn_dt_bias, dn_norm_g, ssm_lam_re, ssm_lam_im, ssm_log_dt, ssm_b_re, ssm_b_im, ssm_c_re, ssm_c_im, ssm_d, ssm_w_glu, ssm_b_glu, gqa_q_norm, gqa_k_norm, mla_q_norm, mla_kv_norm, mla_w_uq, mla_w_ukv, w_out, moe_w_router, moe_b_router, moe_w_gate, moe_b_gate, moe_w_up, moe_b_up, moe_w_down, moe_b_down, norm_f):
    bc, lc, d = x_prompt.shape
    bl, ll, _ = x_sample.shape
    depth = w_in.shape[0]
    m_ctx = bc * lc
    assert m_ctx % ll == 0 and lc % DN_CHUNK == 0 and ll % DN_CHUNK == 0
    lat_blk0 = m_ctx // ll
    gw = d // 4
    nh = dn_a_log.shape[2]
    n_gqa = gw + 2 * GQA_KV * GQA_HD
    n_mla = MLA_Q_LORA + MLA_KV_LORA + MLA_ROPE
    dn_in = 4 * gw + 4 * nh
    assert w_in.shape[2] == dn_in + gw + n_gqa + n_mla

    c_ssm = 4 * gw
    c_gqa = c_ssm + gw
    q0, k0, v0 = c_gqa, c_gqa + gw, c_gqa + gw + GQA_KV * GQA_HD
    c_mla = _round_up(c_gqa + n_gqa, MLA_Q_LORA)
    c_kr = c_mla + MLA_Q_LORA + MLA_KV_LORA
    assert (c_mla + MLA_Q_LORA) % MLA_KV_LORA == 0 and c_kr % LANES == 0 and MLA_ROPE + 4 * nh <= LANES
    n_w = _round_up(c_kr + LANES, 512)
    o_ssm, o_gqa, o_mla = dn_in, dn_in + gw, dn_in + gw + n_gqa
    zpad = lambda n: jnp.zeros((depth, d, n), F32)
    w_in_p = jnp.concatenate(
        [w_in[:, :, :4 * gw], w_in[:, :, o_ssm:o_gqa], w_in[:, :, o_gqa:o_mla], zpad(c_mla - c_gqa - n_gqa),
         w_in[:, :, o_mla:o_mla + MLA_Q_LORA + MLA_KV_LORA],
         w_in[:, :, o_mla + MLA_Q_LORA + MLA_KV_LORA:], zpad(LANES // 2 - MLA_ROPE), w_in[:, :, 4 * gw:dn_in],
         zpad(n_w - c_kr - LANES // 2 - 4 * nh)], axis=-1).astype(BF16)
    w_out_b = w_out.astype(BF16)

    x = jnp.concatenate([x_prompt.reshape(m_ctx, d), x_sample.reshape(bl * ll, d)], axis=0)
    n_cond = _round_up(1 + bl, 8)
    cond = jnp.zeros((n_cond, d), F32).at[0].set(c_ctx).at[1:1 + bl].set(c)

    p_len = cache_gqa_k.shape[2]
    ck = cache_gqa_k.reshape(bl, depth, p_len, GQA_KV * GQA_HD)
    cv = cache_gqa_v.reshape(bl, depth, p_len, GQA_KV * GQA_HD)
    ckr2 = jnp.concatenate([cache_mla_krope, cache_mla_krope], axis=-1)
    n_grp, n_p = ssm_lam_re.shape[2:]

    outs = [[] for _ in range(7)]
    for l in range(depth):
        mod = _ada(cond, w_ada, b_ada[l], l).reshape(n_cond, 6, 1, d)
        xin = _inproj(x, mod, norm1_g[l], w_in_p, l, m_ctx, ll)

        dn_args = (xin, c_kr, dn_conv[l], dn_a_log[l], dn_dt_bias[l], dn_norm_g[l])
        o_dn_c, dn_fin = _deltanet(*dn_args, bc, lc, 0)
        o_dn_l = _deltanet(*dn_args, bl, ll, lat_blk0, state=state_dn, layer=l)

        ops = _s5_operators(ssm_lam_re[l], ssm_lam_im[l], ssm_log_dt[l], ssm_b_re[l], ssm_b_im[l],
                            ssm_c_re[l], ssm_c_im[l], ssm_d[l])
        u = xin[:, c_ssm:c_ssm + gw]
        y_c, ssm_fin = _s5_path(u[:m_ctx], ops, bc, lc, jnp.zeros((n_grp, 2, bc, 2 * n_p), F32))
        h0 = jnp.concatenate([state_ssm_re[:, l], state_ssm_im[:, l]], axis=-1).transpose(2, 1, 0, 3)
        y_l, _ = _s5_path(u[m_ctx:], ops, bl, ll, h0)
        o_s5 = _glu(jnp.concatenate([y_c, y_l], axis=0), ssm_w_glu[l], ssm_b_glu[l])

        gqa_args = (xin, (q0, k0, v0), gqa_q_norm[l], gqa_k_norm[l])
        o_gqa_c, k_new = _gqa(*gqa_args, bc, lc, 0)
        o_gqa_l = _gqa(*gqa_args, bl, ll, lat_blk0, cache=(ck, cv), layer=l)

        mla_args = (xin, c_mla, mla_q_norm[l], mla_kv_norm[l], mla_w_uq[l], mla_w_ukv[l])
        o_mla_c, ckv_new = _mla(*mla_args, bc, lc, 0)
        o_mla_l = _mla(*mla_args, bl, ll, lat_blk0, cache=(cache_mla_ckv, ckr2), layer=l)

        cat = lambda a, b_: jnp.concatenate([a, b_], axis=0)
        parts = [cat(o_dn_c, o_dn_l), o_s5, cat(o_gqa_c, o_gqa_l), cat(o_mla_c, o_mla_l)]
        x = _outproj(parts, w_out_b, l, x, mod, m_ctx, ll)
        x = _moe(x, mod, norm2_g[l], moe_w_router[l], moe_b_router[l], moe_w_gate, moe_b_gate,
                 moe_w_up, moe_b_up, moe_w_down, moe_b_down, l, norm_f, m_ctx, ll,
                 final_norm=(l == depth - 1))

        fin = ssm_fin.transpose(2, 1, 0, 3)
        outs[0].append(k_new.reshape(bc, lc, GQA_KV, GQA_HD))
        outs[1].append(xin[:m_ctx, v0:v0 + GQA_KV * GQA_HD].reshape(bc, lc, GQA_KV, GQA_HD))
        outs[2].append(ckv_new.reshape(bc, lc, MLA_KV_LORA))
        outs[3].append(xin[:m_ctx, c_kr:c_kr + MLA_ROPE].reshape(bc, lc, MLA_ROPE))
        outs[4].append(dn_fin)
        outs[5].append(fin[..., :n_p])
        outs[6].append(fin[..., n_p:])

    y_prompt = x[:m_ctx].reshape(bc, lc, d)
    y_sample = x[m_ctx:].reshape(bl, ll, d)
    return (y_prompt, y_sample) + tuple(jnp.stack(o, axis=1) for o in outs)
```

```python
import functools
import math

import jax
import jax.numpy as jnp
from jax import lax
from jax.experimental import pallas as pl
from jax.experimental.pallas import tpu as pltpu

F32 = jnp.float32
BF16 = jnp.bfloat16
I32 = jnp.int32

EPS = 1e-6
ROPE_THETA = 10000.0
GRID_W = 64
N_EXPERTS = 32
TOP_K = 4
SWIGLU_LIMIT = 7.0
SWIGLU_ALPHA = 1.702
DN_CONV = 5
DN_CHUNK = 64
DN_HEAD_DIM = 128
SSM_CH = 16
SSM_P = 64
SSM_T = 16
GQA_HD = 128
GQA_KV = 2
MLA_NOPE = 128
MLA_ROPE = 64
MLA_V = 128
MLA_Q_LORA = 768
MLA_KV_LORA = 256

LANES = 128
VMEM_LIMIT = 56 * 1024 * 1024
NEG_BIG = -1e30
HIGHEST = lax.Precision.HIGHEST


def _cparams(n_axes):
    return pltpu.CompilerParams(dimension_semantics=("arbitrary",) * n_axes, vmem_limit_bytes=VMEM_LIMIT)


def _tile(n, want):
    t = min(n, want)
    while n % t:
        t //= 2
    return t


def _bdot(a, b):
    return jnp.dot(a.astype(BF16), b.astype(BF16), preferred_element_type=F32)


def _bdot_nt(a, b):
    return lax.dot_general(a.astype(BF16), b.astype(BF16), (((1,), (1,)), ((), ())), preferred_element_type=F32)


def _bdot_tn(a, b):
    return lax.dot_general(a.astype(BF16), b.astype(BF16), (((0,), (0,)), ((), ())), preferred_element_type=F32)


def _split(a):
    hi = a.astype(BF16)
    lo = (a - hi.astype(F32)).astype(BF16)
    return hi, lo


def _dot3(a, b):
    ah, al = _split(a)
    bh, bl = _split(b)
    d = functools.partial(jnp.dot, preferred_element_type=F32)
    return d(ah, bh) + (d(ah, bl) + d(al, bh))


def _sigmoid(x):
    return 1.0 / (1.0 + jnp.exp(-x))


def _silu(x):
    return x * _sigmoid(x)


def _rms(x, g):
    return x * lax.rsqrt(jnp.mean(x * x, axis=-1, keepdims=True) + EPS) * g


def _ada_kernel(c_ref, w_ref, b_ref, o_ref):
    o_ref[...] = _bdot(_silu(c_ref[...]), w_ref[...]) + b_ref[...]


def _ada(cond, w, b, layer):
    r, d = cond.shape
    n = w.shape[2]
    tn = _tile(n, 512)
    return pl.pallas_call(
        _ada_kernel, name="ada",
        out_shape=jax.ShapeDtypeStruct((r, n), F32),
        grid=(n // tn,),
        in_specs=[pl.BlockSpec((r, d), lambda j: (0, 0)),
                  pl.BlockSpec((None, d, tn), lambda j: (layer, 0, j)),
                  pl.BlockSpec((1, tn), lambda j: (0, j))],
        out_specs=pl.BlockSpec((r, tn), lambda j: (0, j)),
        compiler_params=_cparams(1),
    )(cond, w, b.reshape(1, n))


def _mod_row(i, tm, m_ctx, l_lat):
    r0 = i * tm
    return jnp.where(r0 < m_ctx, 0, 1 + (r0 - m_ctx) // l_lat)


def _mod_spec(d, which, tm, m_ctx, l_lat, n_grid):
    if n_grid == 1:
        return pl.BlockSpec((None, None, 1, d), lambda i: (_mod_row(i, tm, m_ctx, l_lat), which, 0, 0))
    return pl.BlockSpec((None, None, 1, d), lambda i, j: (_mod_row(i, tm, m_ctx, l_lat), which, 0, 0))


def _inproj_kernel(x_ref, sh_ref, sc_ref, g_ref, w_ref, o_ref, h_ref):
    @pl.when(pl.program_id(1) == 0)
    def _():
        y = _rms(x_ref[...], g_ref[...])
        h_ref[...] = (y * (1.0 + sc_ref[...]) + sh_ref[...]).astype(BF16)

    o_ref[...] = jnp.dot(h_ref[...], w_ref[...], preferred_element_type=F32)


def _inproj(x, mod4, g, w, layer, m_ctx, l_lat):
    m, d = x.shape
    n = w.shape[2]
    tm = _tile(math.gcd(m_ctx, l_lat), 512)
    tn = _tile(n, 512)
    if n % tn or tn % LANES:
        tn = LANES
    return pl.pallas_call(
        _inproj_kernel, name="inproj",
        out_shape=jax.ShapeDtypeStruct((m, n), F32),
        grid=(m // tm, n // tn),
        in_specs=[pl.BlockSpec((tm, d), lambda i, j: (i, 0)),
                  _mod_spec(d, 0, tm, m_ctx, l_lat, 2),
                  _mod_spec(d, 1, tm, m_ctx, l_lat, 2),
                  pl.BlockSpec((1, d), lambda i, j: (0, 0)),
                  pl.BlockSpec((None, d, tn), lambda i, j: (layer, 0, j))],
        out_specs=pl.BlockSpec((tm, tn), lambda i, j: (i, j)),
        scratch_shapes=[pltpu.VMEM((tm, d), BF16)],
        compiler_params=_cparams(2),
    )(x, mod4, mod4, g.reshape(1, d), w)


def _outproj_kernel(a0, a1, a2, a3, w_ref, x_ref, g_ref, o_ref):
    gw = a0.shape[1]
    acc = jnp.dot(a0[...], w_ref[0:gw, :], preferred_element_type=F32)
    acc += jnp.dot(a1[...], w_ref[gw:2 * gw, :], preferred_element_type=F32)
    acc += jnp.dot(a2[...], w_ref[2 * gw:3 * gw, :], preferred_element_type=F32)
    acc += jnp.dot(a3[...], w_ref[3 * gw:4 * gw, :], preferred_element_type=F32)
    o_ref[...] = x_ref[...] + g_ref[...] * acc


def _outproj(parts, w, layer, x, mod4, m_ctx, l_lat):
    m, d = x.shape
    gw = parts[0].shape[1]
    tm = _tile(math.gcd(m_ctx, l_lat), 512)
    tn = _tile(d, 512)
    part_spec = pl.BlockSpec((tm, gw), lambda i, j: (i, 0))
    g_spec = pl.BlockSpec((None, None, 1, tn), lambda i, j: (_mod_row(i, tm, m_ctx, l_lat), 2, 0, j))
    return pl.pallas_call(
        _outproj_kernel, name="outproj",
        out_shape=jax.ShapeDtypeStruct((m, d), F32),
        grid=(m // tm, d // tn),
        in_specs=[part_spec] * 4 + [pl.BlockSpec((None, 4 * gw, tn), lambda i, j: (layer, 0, j)),
                                    pl.BlockSpec((tm, tn), lambda i, j: (i, j)),
                                    g_spec],
        out_specs=pl.BlockSpec((tm, tn), lambda i, j: (i, j)),
        compiler_params=_cparams(2),
    )(*parts, w, x, mod4)


def _router_kernel(x_ref, sh_ref, sc_ref, g_ref, wr_ref, br_ref, h_ref, idx_ref, gate_ref):
    y = _rms(x_ref[...], g_ref[...])
    h = y * (1.0 + sc_ref[...]) + sh_ref[...]
    half = h.shape[1] // 2
    bits = pltpu.bitcast(h.astype(BF16).astype(F32), jnp.uint32)
    packed = bits[:, half:] | lax.shift_right_logical(bits[:, :half], jnp.uint32(16))
    for c in range(half // LANES):
        h_ref[:, c, :] = packed[:, c * LANES:(c + 1) * LANES]
    logits =jnp.dot(h, wr_ref[...], precision=HIGHEST, preferred_element_type=F32) + br_ref[...]
    lane = lax.broadcasted_iota(I32, logits.shape, 1).astype(F32)
    cur = logits
    vals, idxs = [], []
    for _ in range(TOP_K):
        mx = jnp.max(cur, axis=-1, keepdims=True)
        am = jnp.min(jnp.where(cur == mx, lane, float(LANES)), axis=-1, keepdims=True)
        vals.append(mx)
        idxs.append(am)
        cur = jnp.where(lane == am, -jnp.inf, cur)
    es = [jnp.exp(v - vals[0]) for v in vals]
    tot = es[0] + es[1] + es[2] + es[3]
    idx_o = jnp.zeros(logits.shape, F32)
    gate_o = jnp.zeros(logits.shape, F32)
    for k in range(TOP_K):
        idx_o = jnp.where(lane == float(k), idxs[k], idx_o)
        gate_o = jnp.where(lane == float(k), es[k] / tot, gate_o)
    idx_ref[...] = idx_o.astype(I32)
    gate_ref[...] = gate_o


def _router(x, mod4, g, wr, br, m_ctx, l_lat):
    m, d = x.shape
    tm = _tile(math.gcd(m_ctx, l_lat), 256)
    wr_p = jnp.zeros((d, LANES), F32).at[:, :N_EXPERTS].set(wr)
    br_p = jnp.full((1, LANES), NEG_BIG, F32).at[0, :N_EXPERTS].set(br)
    return pl.pallas_call(
        _router_kernel, name="router",
        out_shape=(jax.ShapeDtypeStruct((m, d // 2 // LANES, LANES), jnp.uint32),
                   jax.ShapeDtypeStruct((m, LANES), I32),
                   jax.ShapeDtypeStruct((m, LANES), F32)),
        grid=(m // tm,),
        in_specs=[pl.BlockSpec((tm, d), lambda i: (i, 0)),
                  _mod_spec(d, 3, tm, m_ctx, l_lat, 1),
                  _mod_spec(d, 4, tm, m_ctx, l_lat, 1),
                  pl.BlockSpec((1, d), lambda i: (0, 0)),
                  pl.BlockSpec((d, LANES), lambda i: (0, 0)),
                  pl.BlockSpec((1, LANES), lambda i: (0, 0))],
        out_specs=(pl.BlockSpec((tm, d // 2 // LANES, LANES), lambda i: (i, 0, 0)),
                   pl.BlockSpec((tm, LANES), lambda i: (i, 0)),
                   pl.BlockSpec((tm, LANES), lambda i: (i, 0))),
        compiler_params=_cparams(1),
    )(x, mod4, mod4, g.reshape(1, d), wr_p, br_p)


def _route_plan(top_idx, tm, n_tiles):
    m = top_idx.shape[0]
    flat_e = top_idx.reshape(-1)
    onehot = (flat_e[:, None] == jnp.arange(N_EXPERTS, dtype=I32)[None, :]).astype(I32)
    csum = jnp.cumsum(onehot, axis=0)
    rank = jnp.take_along_axis(csum, flat_e[:, None], axis=1)[:, 0] - 1
    counts = csum[-1]
    tiles_e = (counts + tm - 1) // tm
    tile_end = jnp.cumsum(tiles_e)
    tile_start = tile_end - tiles_e
    dest = tile_start[flat_e] * tm + rank
    row_tok = jnp.zeros((n_tiles * tm,), I32).at[dest].set(jnp.arange(m * TOP_K, dtype=I32) // TOP_K)
    return dest.astype(I32), row_tok, tiles_e.astype(I32), tile_start.astype(I32), tile_end.astype(I32)


def _schedule(tiles_e, tile_start, tile_end, n_tiles, nj):
    n_used = tile_end[-1]
    step = jnp.arange(n_tiles * nj, dtype=I32)
    s = jnp.minimum(step, n_used * nj - 1)
    e = jnp.minimum(jnp.sum((s[:, None] >= (tile_end * nj)[None, :]).astype(I32), axis=1), N_EXPERTS - 1)
    local = s - tile_start[e] * nj
    t_e = jnp.maximum(tiles_e[e], 1)
    j = (local // t_e).astype(I32)
    r = local - j * t_e
    tile = (tile_start[e] + r).astype(I32)
    valid = step < n_used * nj
    spare = step - n_used * nj
    out_tile = jnp.where(valid, tile, n_used + spare // nj).astype(I32)
    out_j = jnp.where(valid, j, spare % nj).astype(I32)
    flags = jnp.where(valid, 1 + 2 * (r == 0).astype(I32), 4).astype(I32)
    return e, j, tile, out_tile, out_j, flags


def _gather_kernel(tok_ref, nused_ref, h_hbm, o_ref, buf, sem):
    i = pl.program_id(0)
    tg = o_ref.shape[0]
    n_c = h_hbm.shape[1]

    @pl.when(i < nused_ref[0])
    def _():
        def row_copy(t, r):
            return pltpu.make_async_copy(h_hbm.at[t], buf.at[pl.ds(pl.multiple_of(r * n_c, n_c), n_c), :], sem)

        def issue(r2, c):
            for k in range(2):
                r = 2 * r2 + k
                row_copy(tok_ref[i * tg + r], r).start(priority=k)
            return c

        lax.fori_loop(0, tg // 2, issue, 0, unroll=4)

        def wait(r, c):
            row_copy(0, r).wait()
            return c

        lax.fori_loop(0, tg, wait, 0, unroll=8)
        half = n_c * LANES
        for c in range(n_c):
            w = buf[pl.ds(c, tg, stride=n_c), :]
            o_ref[:, c * LANES:(c + 1) * LANES] = pltpu.bitcast(lax.shift_left(w, jnp.uint32(16)), F32).astype(BF16)
            o_ref[:, half + c * LANES:half + (c + 1) * LANES] = pltpu.bitcast(
                w & jnp.uint32(0xFFFF0000), F32).astype(BF16)

    @pl.when(i >= nused_ref[0])
    def _():
        o_ref[...] = jnp.zeros(o_ref.shape, o_ref.dtype)


def _dispatch(h, row_tok, n_used_rows_tiles, tg):
    d = 2 * h.shape[1] * h.shape[2]
    n_rows = row_tok.shape[0]
    grid_spec = pltpu.PrefetchScalarGridSpec(
        num_scalar_prefetch=2, grid=(n_rows // tg,),
        in_specs=[pl.BlockSpec(memory_space=pl.ANY)],
        out_specs=pl.BlockSpec((tg, d), lambda i, tok, nu: (i, 0)),
        scratch_shapes=[pltpu.VMEM((tg * h.shape[1], h.shape[2]), jnp.uint32), pltpu.SemaphoreType.DMA(())])
    return pl.pallas_call(
        _gather_kernel, name="dispatch", grid_spec=grid_spec,
        out_shape=jax.ShapeDtypeStruct((n_rows, d), BF16),
        compiler_params=_cparams(1),
    )(row_tok, n_used_rows_tiles, h)


def _moe_up_kernel(se, sj, st, so, soj, sf, x_ref, wg_ref, wu_ref, bg_ref, bu_ref, o_ref, wgb, wub):
    flag = sf[pl.program_id(0)]

    @pl.when((flag & 4) != 0)
    def _():
        o_ref[...] = jnp.zeros(o_ref.shape, o_ref.dtype)

    @pl.when((flag & 2) != 0)
    def _():
        wgb[...] = wg_ref[...].astype(BF16)
        wub[...] = wu_ref[...].astype(BF16)

    @pl.when((flag & 1) != 0)
    def _():
        x = x_ref[...]
        gate = jnp.dot(x, wgb[...], preferred_element_type=F32) + bg_ref[...]
        up = jnp.dot(x, wub[...], preferred_element_type=F32) + bu_ref[...]
        gate = jnp.minimum(gate, SWIGLU_LIMIT)
        up = jnp.clip(up, -SWIGLU_LIMIT, SWIGLU_LIMIT)
        o_ref[...] = ((up + 1.0) * gate * _sigmoid(SWIGLU_ALPHA * gate)).astype(BF16)


def _moe_up(xs, wg, wu, bg, bu, layer, sched, tm, tn):
    n_rows, d = xs.shape
    depth, e, _, f = wg.shape
    n_steps = sched[0].shape[0]
    w_spec = pl.BlockSpec((None, None, d, tn), lambda s, se, sj, st, so, soj, sf: (layer, se[s], 0, sj[s]))
    b_spec = pl.BlockSpec((None, None, 1, tn), lambda s, se, sj, st, so, soj, sf: (layer, se[s], 0, sj[s]))
    grid_spec = pltpu.PrefetchScalarGridSpec(
        num_scalar_prefetch=6, grid=(n_steps,),
        in_specs=[pl.BlockSpec((tm, d), lambda s, se, sj, st, so, soj, sf: (st[s], 0)),
                  w_spec, w_spec, b_spec, b_spec],
        out_specs=pl.BlockSpec((tm, tn), lambda s, se, sj, st, so, soj, sf: (so[s], soj[s])),
        scratch_shapes=[pltpu.VMEM((d, tn), BF16), pltpu.VMEM((d, tn), BF16)])
    return pl.pallas_call(
        _moe_up_kernel, name="moe_up", grid_spec=grid_spec,
        out_shape=jax.ShapeDtypeStruct((n_rows, f), BF16),
        compiler_params=_cparams(1),
    )(*sched, xs, wg, wu, bg.reshape(depth, e, 1, f), bu.reshape(depth, e, 1, f))


def _moe_down_kernel(se, sj, st, so, soj, sf, h_ref, wd_ref, bd_ref, o_ref, wdb):
    flag = sf[pl.program_id(0)]

    @pl.when((flag & 4) != 0)
    def _():
        o_ref[...] = jnp.zeros(o_ref.shape, o_ref.dtype)

    @pl.when((flag & 2) != 0)
    def _():
        wdb[...] = wd_ref[...].astype(BF16)

    @pl.when((flag & 1) != 0)
    def _():
        o_ref[...] = jnp.dot(h_ref[...], wdb[...], preferred_element_type=F32) + bd_ref[...]


def _moe_down(hid, wd, bd, layer, sched, tm, tn):
    n_rows, f = hid.shape
    depth, e, _, d = wd.shape
    n_steps = sched[0].shape[0]
    grid_spec = pltpu.PrefetchScalarGridSpec(
        num_scalar_prefetch=6, grid=(n_steps,),
        in_specs=[pl.BlockSpec((tm, f), lambda s, se, sj, st, so, soj, sf: (st[s], 0)),
                  pl.BlockSpec((None, None, f, tn), lambda s, se, sj, st, so, soj, sf: (layer, se[s], 0, sj[s])),
                  pl.BlockSpec((None, None, 1, tn), lambda s, se, sj, st, so, soj, sf: (layer, se[s], 0, sj[s]))],
        out_specs=pl.BlockSpec((tm, tn), lambda s, se, sj, st, so, soj, sf: (so[s], soj[s])),
        scratch_shapes=[pltpu.VMEM((f, tn), BF16)])
    return pl.pallas_call(
        _moe_down_kernel, name="moe_down", grid_spec=grid_spec,
        out_shape=jax.ShapeDtypeStruct((n_rows, d), F32),
        compiler_params=_cparams(1),
    )(*sched, hid, wd, bd.reshape(depth, e, 1, d))


def _combine_kernel(pos_ref, y_hbm, x_ref, gate_ref, g2_ref, nf_ref, o_ref, buf, sem, *, final_norm):
    i = pl.program_id(0)
    tc = x_ref.shape[0]

    def issue(t, c):
        for k in range(TOP_K):
            p = pos_ref[(i * tc + t) * TOP_K + k]
            pltpu.make_async_copy(y_hbm.at[pl.ds(p, 1)], buf.at[k, pl.ds(t, 1)], sem).start(priority=k % 2)
        return c

    lax.fori_loop(0, tc, issue, 0)

    def wait(t, c):
        for k in range(TOP_K):
            pltpu.make_async_copy(y_hbm.at[pl.ds(0, 1)], buf.at[k, pl.ds(t, 1)], sem).wait()
        return c

    lax.fori_loop(0, tc, wait, 0)
    gates = gate_ref[...]
    acc = gates[:, 0:1] * buf[0]
    for k in range(1, TOP_K):
        acc += gates[:, k:k + 1] * buf[k]
    out = x_ref[...] + g2_ref[...] * acc
    if final_norm:
        out = _rms(out, nf_ref[...])
    o_ref[...] = out


def _combine(y, pos, x, gates, mod4, norm_f, m_ctx, l_lat, final_norm):
    m, d = x.shape
    tc = _tile(math.gcd(m_ctx, l_lat), 128)
    grid_spec = pltpu.PrefetchScalarGridSpec(
        num_scalar_prefetch=1, grid=(m // tc,),
        in_specs=[pl.BlockSpec(memory_space=pl.ANY),
                  pl.BlockSpec((tc, d), lambda i, pos: (i, 0)),
                  pl.BlockSpec((tc, LANES), lambda i, pos: (i, 0)),
                  pl.BlockSpec((None, None, 1, d), lambda i, pos: (_mod_row(i, tc, m_ctx, l_lat), 5, 0, 0)),
                  pl.BlockSpec((1, d), lambda i, pos: (0, 0))],
        out_specs=pl.BlockSpec((tc, d), lambda i, pos: (i, 0)),
        scratch_shapes=[pltpu.VMEM((TOP_K, tc, d), F32), pltpu.SemaphoreType.DMA(())])
    return pl.pallas_call(
        functools.partial(_combine_kernel, final_norm=final_norm), name="combine", grid_spec=grid_spec,
        out_shape=jax.ShapeDtypeStruct((m, d), F32),
        compiler_params=_cparams(1),
    )(pos, y, x, gates, mod4, norm_f.reshape(1, d))


def _moe(x, mod4, norm2_g, wr, br, wg, bg, wu, bu, wd, bd, layer, norm_f, m_ctx, l_lat, final_norm):
    m, d = x.shape
    f = wg.shape[3]
    tm = _tile(m, 512)
    n_tiles = -(-(m * TOP_K + N_EXPERTS * (tm - 1)) // tm)
    h, top_idx, gates = _router(x, mod4, norm2_g, wr, br, m_ctx, l_lat)
    dest, row_tok, tiles_e, tile_start, tile_end = _route_plan(top_idx[:, :TOP_K], tm, n_tiles)
    xs = _dispatch(h, row_tok, tile_end[-1:], tm)
    tn_up = _tile(f, 512)
    sched_up = _schedule(tiles_e, tile_start, tile_end, n_tiles, f // tn_up)
    hid = _moe_up(xs, wg, wu, bg, bu, layer, sched_up, tm, tn_up)
    tn_dn = _tile(d, 1024)
    sched_dn = _schedule(tiles_e, tile_start, tile_end, n_tiles, d // tn_dn)
    y = _moe_down(hid, wd, bd, layer, sched_dn, tm, tn_dn)
    return _combine(y, dest, x, gates, mod4, norm_f, m_ctx, l_lat, final_norm)


def _rope_tables(l, d):
    t = jnp.arange(l, dtype=I32)
    row = (t // GRID_W).astype(F32)[:, None]
    col = (t % GRID_W).astype(F32)[:, None]
    q = d // 4
    lane = jnp.arange(LANES, dtype=I32) % d
    pos = jnp.where((lane // (2 * q))[None, :] == 0, row, col)
    inv = ROPE_THETA ** (-(lane % q).astype(F32) / q)
    ang = pos * inv[None, :]
    sign = jnp.where((lane % (2 * q)) < q, -1.0, 1.0)[None, :]
    return jnp.cos(ang), jnp.sin(ang) * sign


def _rope(x, cos, sin, d):
    q = d // 4
    lane = lax.broadcasted_iota(I32, x.shape, 1)
    fwd = pltpu.roll(x, LANES - q, 1)
    bwd = pltpu.roll(x, q, 1)
    return x * cos + jnp.where((lane % (2 * q)) < q, fwd, bwd) * sin


def _attend(q_scr, k_scr, v_scr, o_ref, col0, tq):
    dv = v_scr.shape[1]

    def body(i, c):
        r0 = pl.multiple_of(i * tq, tq)
        s = lax.dot_general(q_scr[pl.ds(r0, tq), :], k_scr[...], (((1,), (1,)), ((), ())),
                            preferred_element_type=F32)
        e = jnp.exp(s - jnp.max(s, axis=-1, keepdims=True))
        o = jnp.dot(e.astype(BF16), v_scr[...], preferred_element_type=F32)
        o = o * (1.0 / jnp.sum(e, axis=-1, keepdims=True))
        o_ref[pl.ds(r0, tq), col0:col0 + dv] = o.astype(o_ref.dtype)
        return c

    lax.fori_loop(0, q_scr.shape[0] // tq, body, 0)


def _gqa_kernel(*refs, has_cache, grp):
    if has_cache:
        q_ref, k_ref, v_ref, qn_ref, kn_ref, ck_ref, cv_ref, cos_ref, sin_ref, o_ref, q_scr, k_scr, v_scr = refs
    else:
        q_ref, k_ref, v_ref, qn_ref, kn_ref, o_ref, kout_ref, q_scr, k_scr, v_scr = refs
    l = k_ref.shape[0]
    k = _rms(k_ref[...], kn_ref[...])
    if has_cache:
        p = ck_ref.shape[0]
        k_scr[0:p, :] = ck_ref[...].astype(BF16)
        v_scr[0:p, :] = cv_ref[...].astype(BF16)
        k_scr[p:p + l, :] = _rope(k, cos_ref[...], sin_ref[...], GQA_HD).astype(BF16)
        v_scr[p:p + l, :] = v_ref[...].astype(BF16)
    else:
        kout_ref[...] = k
        k_scr[...] = k.astype(BF16)
        v_scr[...] = v_ref[...].astype(BF16)
    scale = GQA_HD ** -0.5
    for g in range(grp):
        q = _rms(q_ref[:, g * GQA_HD:(g + 1) * GQA_HD], qn_ref[...])
        if has_cache:
            q = _rope(q, cos_ref[...], sin_ref[...], GQA_HD)
        q_scr[...] = (q * scale).astype(BF16)
        _attend(q_scr, k_scr, v_scr, o_ref, g * GQA_HD, _tile(l, 256))


def _gqa(xin, cols, qn, kn, b, l, row_blk0, cache=None, layer=0):
    q0, k0, v0 = cols
    n_q = k0 - q0
    grp = n_q // GQA_KV // GQA_HD
    qw = grp * GQA_HD
    has_cache = cache is not None
    in_specs = [pl.BlockSpec((l, qw), lambda i, h: (row_blk0 + i, q0 // qw + h)),
                pl.BlockSpec((l, GQA_HD), lambda i, h: (row_blk0 + i, k0 // GQA_HD + h)),
                pl.BlockSpec((l, GQA_HD), lambda i, h: (row_blk0 + i, v0 // GQA_HD + h)),
                pl.BlockSpec((1, GQA_HD), lambda i, h: (0, 0)),
                pl.BlockSpec((1, GQA_HD), lambda i, h: (0, 0))]
    args = [xin, xin, xin, qn.reshape(1, GQA_HD), kn.reshape(1, GQA_HD)]
    lk = l
    o_shape = jax.ShapeDtypeStruct((b * l, n_q), BF16)
    o_spec = pl.BlockSpec((l, qw), lambda i, h: (i, h))
    if has_cache:
        ck, cv = cache
        p = ck.shape[2]
        lk = l + p
        cspec = pl.BlockSpec((None, None, p, GQA_HD), lambda i, h: (i, layer, 0, h))
        tspec = pl.BlockSpec((l, LANES), lambda i, h: (0, 0))
        cos, sin = _rope_tables(l, GQA_HD)
        in_specs += [cspec, cspec, tspec, tspec]
        args += [ck, cv, cos, sin]
        out_shape, out_specs = o_shape, o_spec
    else:
        out_shape = (o_shape, jax.ShapeDtypeStruct((b * l, GQA_KV * GQA_HD), F32))
        out_specs = (o_spec, pl.BlockSpec((l, GQA_HD), lambda i, h: (i, h)))
    return pl.pallas_call(
        functools.partial(_gqa_kernel, has_cache=has_cache, grp=grp), name="gqa_lat" if has_cache else "gqa_ctx",
        out_shape=out_shape, grid=(b, GQA_KV), in_specs=in_specs, out_specs=out_specs,
        scratch_shapes=[pltpu.VMEM((l, GQA_HD), BF16), pltpu.VMEM((lk, GQA_HD), BF16), pltpu.VMEM((lk, GQA_HD), BF16)],
        compiler_params=_cparams(2),
    )(*args)


def _mla_kernel(*refs, has_cache, heads):
    if has_cache:
        (qc_ref, ckv_ref, kr_ref, qn_ref, kvn_ref, wqn_ref, wqr_ref, wuk_ref, wuv_ref, cc_ref, ckr_ref, cos_ref,
         sin_ref, o_ref, qn_scr, qr_scr, kn_scr, vv_scr, kr_scr) = refs
    else:
        (qc_ref, ckv_ref, kr_ref, qn_ref, kvn_ref, wqn_ref, wqr_ref, wuk_ref, wuv_ref, o_ref, ckv_out_ref,
         qn_scr, qr_scr, kn_scr, vv_scr, kr_scr) = refs
    l = qc_ref.shape[0]
    scale = (MLA_NOPE + MLA_ROPE) ** -0.5
    qc = _rms(qc_ref[...], qn_ref[...]).astype(BF16)
    qn_scr[...] = jnp.dot(qc, wqn_ref[...], preferred_element_type=F32) * scale
    qr = jnp.dot(qc, wqr_ref[...], preferred_element_type=F32) * scale
    ckv = _rms(ckv_ref[...], kvn_ref[...])
    kr = kr_ref[...]
    lane = lax.broadcasted_iota(I32, kr.shape, 1)
    kr = jnp.where(lane < MLA_ROPE, kr, pltpu.roll(kr, MLA_ROPE, 1))
    if has_cache:
        p = cc_ref.shape[0]
        cos, sin = cos_ref[...], sin_ref[...]
        for c in range(qr.shape[1] // LANES):
            qr_scr[:, c * LANES:(c + 1) * LANES] = _rope(qr[:, c * LANES:(c + 1) * LANES], cos, sin, MLA_ROPE)
        kr_scr[0:p, :] = ckr_ref[...].astype(BF16)
        kr_scr[p:p + l, :] = _rope(kr, cos, sin, MLA_ROPE).astype(BF16)
        cc = cc_ref[...].astype(BF16)
        kn_scr[0:p, :] = jnp.dot(cc, wuk_ref[...], preferred_element_type=F32).astype(BF16)
        vv_scr[0:p, :] = jnp.dot(cc, wuv_ref[...], preferred_element_type=F32).astype(BF16)
    else:
        p = 0
        qr_scr[...] = qr
        kr_scr[...] = kr.astype(BF16)
        ckv_out_ref[...] = ckv
    cb = ckv.astype(BF16)
    kn_scr[p:p + l, :] = jnp.dot(cb, wuk_ref[...], preferred_element_type=F32).astype(BF16)
    vv_scr[p:p + l, :] = jnp.dot(cb, wuv_ref[...], preferred_element_type=F32).astype(BF16)
    tq = _tile(l, 256)
    nt = (((1,), (1,)), ((), ()))
    for h in range(heads):
        c = (h * MLA_ROPE) // LANES

        def tile(t, carry, h=h, c=c):
            rows = slice(0, tq) if l == tq else pl.ds(pl.multiple_of(t * tq, tq), tq)
            qrh = qr_scr[rows, c * LANES:(c + 1) * LANES]
            lq = lax.broadcasted_iota(I32, qrh.shape, 1)
            own = (lq // MLA_ROPE) == (h % (LANES // MLA_ROPE))
            s = lax.dot_general(qn_scr[rows, h * MLA_NOPE:(h + 1) * MLA_NOPE].astype(BF16),
                                kn_scr[:, h * MLA_NOPE:(h + 1) * MLA_NOPE], nt, preferred_element_type=F32)
            s += lax.dot_general(jnp.where(own, qrh, 0.0).astype(BF16), kr_scr[...], nt, preferred_element_type=F32)
            e = jnp.exp(s - jnp.max(s, axis=-1, keepdims=True))
            o = jnp.dot(e.astype(BF16), vv_scr[:, h * MLA_V:(h + 1) * MLA_V], preferred_element_type=F32)
            o = o * (1.0 / jnp.sum(e, axis=-1, keepdims=True))
            o_ref[rows, h * MLA_V:(h + 1) * MLA_V] = o.astype(o_ref.dtype)
            return carry

        if l == tq:
            tile(0, 0)
        else:
            lax.fori_loop(0, l // tq, tile, 0)


def _mla(xin, col0, qn, kvn, w_uq, w_ukv, b, l, row_blk0, cache=None, layer=0):
    heads = w_uq.shape[1] // (MLA_NOPE + MLA_ROPE)
    wq = w_uq.reshape(MLA_Q_LORA, heads, MLA_NOPE + MLA_ROPE)
    wqn = wq[:, :, :MLA_NOPE].reshape(MLA_Q_LORA, heads * MLA_NOPE).astype(BF16)
    wqr = wq[:, :, MLA_NOPE:].reshape(MLA_Q_LORA, heads * MLA_ROPE).astype(BF16)
    wkv = w_ukv.reshape(MLA_KV_LORA, heads, MLA_NOPE + MLA_V)
    wuk = wkv[:, :, :MLA_NOPE].reshape(MLA_KV_LORA, heads * MLA_NOPE).astype(BF16)
    wuv = wkv[:, :, MLA_NOPE:].reshape(MLA_KV_LORA, heads * MLA_V).astype(BF16)
    has_cache = cache is not None
    c_kv = col0 + MLA_Q_LORA
    c_kr = c_kv + MLA_KV_LORA
    full = lambda a: pl.BlockSpec(a.shape, lambda i: (0,) * a.ndim)
    in_specs = [pl.BlockSpec((l, MLA_Q_LORA), lambda i: (row_blk0 + i, col0 // MLA_Q_LORA)),
                pl.BlockSpec((l, MLA_KV_LORA), lambda i: (row_blk0 + i, c_kv // MLA_KV_LORA)),
                pl.BlockSpec((l, LANES), lambda i: (row_blk0 + i, c_kr // LANES)),
                pl.BlockSpec((1, MLA_Q_LORA), lambda i: (0, 0)),
                pl.BlockSpec((1, MLA_KV_LORA), lambda i: (0, 0)),
                full(wqn), full(wqr), full(wuk), full(wuv)]
    args = [xin, xin, xin, qn.reshape(1, -1), kvn.reshape(1, -1), wqn, wqr, wuk, wuv]
    o_shape = jax.ShapeDtypeStruct((b * l, heads * MLA_V), BF16)
    o_spec = pl.BlockSpec((l, heads * MLA_V), lambda i: (i, 0))
    lk = l
    if has_cache:
        cc, ckr2 = cache
        p = cc.shape[2]
        lk = l + p
        cos, sin = _rope_tables(l, MLA_ROPE)
        in_specs += [pl.BlockSpec((None, None, p, MLA_KV_LORA), lambda i: (i, layer, 0, 0)),
                     pl.BlockSpec((None, None, p, LANES), lambda i: (i, layer, 0, 0)),
                     pl.BlockSpec((l, LANES), lambda i: (0, 0)), pl.BlockSpec((l, LANES), lambda i: (0, 0))]
        args += [cc, ckr2, cos, sin]
        out_shape, out_specs = o_shape, o_spec
    else:
        out_shape = (o_shape, jax.ShapeDtypeStruct((b * l, MLA_KV_LORA), F32))
        out_specs = (o_spec, pl.BlockSpec((l, MLA_KV_LORA), lambda i: (i, 0)))
    scratch = [pltpu.VMEM((l, heads * MLA_NOPE), F32), pltpu.VMEM((l, heads * MLA_ROPE), F32),
               pltpu.VMEM((lk, heads * MLA_NOPE), BF16), pltpu.VMEM((lk, heads * MLA_V), BF16),
               pltpu.VMEM((lk, LANES), BF16)]
    return pl.pallas_call(
        functools.partial(_mla_kernel, has_cache=has_cache, heads=heads), name="mla_lat" if has_cache else "mla_ctx",
        out_shape=out_shape, grid=(b,), in_specs=in_specs, out_specs=out_specs, scratch_shapes=scratch,
        compiler_params=_cparams(1),
    )(*args)


def _cmul(ar, ai, br, bi):
    return ar * br - ai * bi, ar * bi + ai * br


def _s5_operators(lam_re, lam_im, log_dt, b_re, b_im, c_re, c_im, d_skip):
    g, p = lam_re.shape[1:]
    t = SSM_T
    n = jnp.arange(t + 1, dtype=F32)[:, None, None]
    us, ws, ks, a_rows = [], [], [], []
    for d in range(2):
        lr, li = lam_re[d], lam_im[d]
        dt = jnp.exp(log_dt[d])[:, None]
        pw_re = jnp.exp(n * (lr * dt)[None]) * jnp.cos(n * (li * dt)[None])
        pw_im = jnp.exp(n * (lr * dt)[None]) * jnp.sin(n * (li * dt)[None])
        den = lr * lr + li * li
        cf_re, cf_im = _cmul(pw_re[1] - 1.0, pw_im[1], lr / den, -li / den)
        bb_re, bb_im = _cmul(cf_re[..., None], cf_im[..., None], b_re[d], b_im[d])
        cl_re, cl_im = _cmul(c_re[None], c_im[None], pw_re[:t, :, None, :], pw_im[:t, :, None, :])
        ks.append(jnp.einsum("ngcp,gpe->ngce", cl_re, bb_re, precision=HIGHEST)
                  - jnp.einsum("ngcp,gpe->ngce", cl_im, bb_im, precision=HIGHEST))
        e = jnp.arange(t)[::-1] if d == 0 else jnp.arange(t)
        u_re, u_im = _cmul(pw_re[e][:, :, None, :], pw_im[e][:, :, None, :],
                           jnp.swapaxes(bb_re, 1, 2)[None], jnp.swapaxes(bb_im, 1, 2)[None])
        us += [jnp.moveaxis(u_re, 1, 0).reshape(g, t * SSM_CH, p), jnp.moveaxis(u_im, 1, 0).reshape(g, t * SSM_CH, p)]
        e = jnp.arange(1, t + 1) if d == 0 else jnp.arange(t, 0, -1)
        w_re, w_im = _cmul(c_re[None], c_im[None], pw_re[e][:, :, None, :], pw_im[e][:, :, None, :])
        to_w = lambda a: jnp.transpose(a, (1, 3, 0, 2)).reshape(g, p, t * SSM_CH)
        ws += [to_w(w_re), -to_w(w_im)]
        a_rows += [jnp.concatenate([pw_re[t], pw_re[t]], -1), jnp.concatenate([-pw_im[t], pw_im[t]], -1)]
    s_in = jnp.arange(t)[:, None]
    s_out = jnp.arange(t)[None, :]
    kf = jnp.where((s_out >= s_in)[..., None, None, None], ks[0][jnp.clip(s_out - s_in, 0, t - 1)], 0.0)
    kb = jnp.where((s_in >= s_out)[..., None, None, None], ks[1][jnp.clip(s_in - s_out, 0, t - 1)], 0.0)
    skip = (s_in == s_out)[..., None, None, None] * (d_skip.reshape(g, SSM_CH)[None, None, :, :, None]
                                                     * jnp.eye(SSM_CH, dtype=F32)[None, None, None])
    tg = jnp.transpose(kf + kb + skip, (2, 0, 4, 1, 3)).reshape(g, t * SSM_CH, t * SSM_CH)
    u_all = jnp.concatenate(us, axis=-1)
    w_all = jnp.concatenate([tg] + ws, axis=1)
    a_all = jnp.stack(a_rows + a_rows, axis=1)
    return u_all, w_all, a_all


def _gelu_tanh(x):
    return 0.5 * x * (1.0 + jnp.tanh(0.7978845608028654 * (x + 0.044715 * (x * x * x))))


def _s5_kernel(x_ref, u_ref, w_ref, a_ref, h0_ref, y_ref, fin_ref, v_scr, hp_scr, gn_scr, *, nb):
    rows = x_ref.shape[0]
    nj = rows // nb
    x = x_ref[...]
    v_scr[...] = jnp.dot(x, u_ref[...], precision=HIGHEST, preferred_element_type=F32)
    two_p = a_ref.shape[1]
    a = a_ref[...]

    def step(j, carry):
        h, g = carry
        rf = pl.multiple_of(j * nb, nb)
        rb = pl.multiple_of((nj - 1 - j) * nb, nb)
        hp_scr[pl.ds(rf, nb), :] = h
        gn_scr[pl.ds(rb, nb), :] = g
        h = a[0:1] * h + a[1:2] * pltpu.roll(h, two_p // 2, 1) + v_scr[pl.ds(rf, nb), 0:two_p]
        g = a[2:3] * g + a[3:4] * pltpu.roll(g, two_p // 2, 1) + v_scr[pl.ds(rb, nb), two_p:2 * two_p]
        return h, g

    h_fin, g_fin = lax.fori_loop(0, nj, step, (h0_ref[0], h0_ref[1]), unroll=2)
    fin_ref[0] = h_fin
    fin_ref[1] = g_fin
    tc = x.shape[1]
    y = jnp.dot(x, w_ref[0:tc, :], precision=HIGHEST, preferred_element_type=F32)
    y += jnp.dot(hp_scr[...], w_ref[tc:tc + two_p, :], precision=HIGHEST, preferred_element_type=F32)
    y += jnp.dot(gn_scr[...], w_ref[tc + two_p:tc + 2 * two_p, :], precision=HIGHEST, preferred_element_type=F32)
    y_ref[...] = _gelu_tanh(y)


def _s5_scan(xg, u_all, w_all, a_all, h0, nb):
    g, rows, tc = xg.shape
    two_p = a_all.shape[2]
    blk = lambda a: pl.BlockSpec((None,) + a.shape[1:], lambda i: (i,) + (0,) * (a.ndim - 1))
    return pl.pallas_call(
        functools.partial(_s5_kernel, nb=nb), name="s5",
        out_shape=(jax.ShapeDtypeStruct((g, rows, tc), F32), jax.ShapeDtypeStruct((g, 2, nb, two_p), F32)),
        grid=(g,),
        in_specs=[blk(xg), blk(u_all), blk(w_all), blk(a_all), blk(h0)],
        out_specs=(pl.BlockSpec((None, rows, tc), lambda i: (i, 0, 0)),
                   pl.BlockSpec((None, 2, nb, two_p), lambda i: (i, 0, 0, 0))),
        scratch_shapes=[pltpu.VMEM((rows, 2 * two_p), F32), pltpu.VMEM((rows, two_p), F32),
                        pltpu.VMEM((rows, two_p), F32)],
        compiler_params=_cparams(1),
    )(xg, u_all, w_all, a_all, h0)


def _glu_kernel(y_ref, w_ref, b_ref, o_ref):
    y = y_ref[...]
    z = jnp.dot(y.astype(BF16), w_ref[...], preferred_element_type=F32) + b_ref[...]
    o_ref[...] = (y * _sigmoid(z)).astype(o_ref.dtype)


def _glu(y, w, b):
    m, n = y.shape
    tm = _tile(m, 512)
    return pl.pallas_call(
        _glu_kernel, name="s5_glu",
        out_shape=jax.ShapeDtypeStruct((m, n), BF16),
        grid=(m // tm,),
        in_specs=[pl.BlockSpec((tm, n), lambda i: (i, 0)), pl.BlockSpec((n, n), lambda i: (0, 0)),
                  pl.BlockSpec((1, n), lambda i: (0, 0))],
        out_specs=pl.BlockSpec((tm, n), lambda i: (i, 0)),
        compiler_params=_cparams(1),
    )(y, w.astype(BF16), b.reshape(1, n))


def _s5_path(u, ops, b, l, h0):
    g = ops[0].shape[0]
    nj = l // SSM_T
    xg = u.reshape(b, nj, SSM_T, g, SSM_CH).transpose(3, 1, 0, 2, 4).reshape(g, nj * b, SSM_T * SSM_CH)
    yg, fin = _s5_scan(xg, *ops, h0, b)
    y = yg.reshape(g, nj, b, SSM_T, SSM_CH).transpose(2, 1, 3, 0, 4).reshape(b * l, g * SSM_CH)
    return y, fin


def _softplus(x):
    return jnp.maximum(x, 0.0) + jnp.log1p(jnp.exp(-jnp.abs(x)))


def _dn_conv(x_ref, w_ref):
    x = x_ref[...]
    l = x.shape[0]
    w = w_ref[...]
    t = lax.broadcasted_iota(I32, x.shape, 0)
    pad = (DN_CONV - 1) // 2
    acc = x * w[pad:pad + 1]
    for tau in range(DN_CONV):
        delta = tau - pad
        if delta == 0:
            continue
        shifted = pltpu.roll(x, (-delta) % l, 0)
        ok = (t + delta >= 0) & (t + delta < l)
        acc += jnp.where(ok, shifted, 0.0) * w[tau:tau + 1]
    return _silu(acc)


def _l2(x):
    return x * lax.rsqrt(jnp.sum(x * x, axis=-1, keepdims=True) + EPS)


def _dn_intra(ci, q_scr, k_scr, v_scr, gb_scr, u_scr, w_scr, qd_scr, kd_scr, qk_scr, et_scr):
    c = DN_CHUNK
    hd = DN_HEAD_DIM
    ri = lax.broadcasted_iota(I32, (c, c), 0)
    cj = lax.broadcasted_iota(I32, (c, c), 1)
    eye = jnp.where(ri == cj, 1.0, 0.0)
    incl = [(cj <= ri), (cj >= ri)]
    strict = [(cj < ri), (cj > ri)]
    rows, one = [], []
    for i in ci:
        static = isinstance(i, int)
        rows.append(slice(i * c, (i + 1) * c) if static else pl.ds(pl.multiple_of(i * c, c), c))
        one.append(slice(i, i + 1) if static else pl.ds(i, 1))
    n = len(ci)
    pairs = [(t, d) for t in range(n) for d in range(2)]
    q = [q_scr[r, :] for r in rows]
    k = [k_scr[r, :] for r in rows]
    kk = [_bdot_nt(k[t], k[t]) for t in range(n)]
    qk_raw = [_bdot_nt(q[t], k[t]) for t in range(n)]
    g = [gb_scr[rows[t], 2 * d * hd:(2 * d + 1) * hd] for t, d in pairs]
    beta = [gb_scr[rows[t], (2 * d + 1) * hd:(2 * d + 2) * hd] for t, d in pairs]
    gc = [jnp.dot(incl[d].astype(F32), g[p], precision=HIGHEST, preferred_element_type=F32)
          for p, (t, d) in enumerate(pairs)]
    lane = lax.broadcasted_iota(I32, (c, hd), 1)
    ones = jnp.ones((c, hd), F32)
    gc_row = [lax.dot_general(ones, jnp.where(lane == 0, x, 0.0), (((1,), (1,)), ((), ())),
                              precision=HIGHEST, preferred_element_type=F32) for x in gc]
    dmat = [jnp.exp(jnp.where(incl[d], gc[p][:, 0:c] - gc_row[p], NEG_BIG)) for p, (t, d) in enumerate(pairs)]
    a = [jnp.where(strict[d], beta[p][:, 0:c] * kk[t] * dmat[p], 0.0) for p, (t, d) in enumerate(pairs)]
    inv = [eye - x for x in a]
    pw = [_dot3(x, x) for x in a]
    for r in range(int(math.log2(c)) - 1):
        inv = [x + _dot3(x, y) for x, y in zip(inv, pw)]
        if r + 2 < int(math.log2(c)):
            pw = [_dot3(y, y) for y in pw]
    egc = [jnp.exp(x) for x in gc]
    tot = [gc[p][c - 1:c, :] if d == 0 else gc[p][0:1, :] for p, (t, d) in enumerate(pairs)]
    u = [_dot3(inv[p], v_scr[rows[t], :] * beta[p]) for p, (t, d) in enumerate(pairs)]
    w = [_dot3(inv[p], k[t] * beta[p] * egc[p]) for p, (t, d) in enumerate(pairs)]
    for p, (t, d) in enumerate(pairs):
        u_scr[d, rows[t], :] = u[p]
        w_scr[d, rows[t], :] = w[p]
        qd_scr[d, rows[t], :] = q[t] * egc[p]
        kd_scr[d, rows[t], :] = k[t] * jnp.exp(tot[p] - gc[p])
        qk_scr[d, rows[t], :] = qk_raw[t] * dmat[p]
        et_scr[d, one[t], :] = jnp.exp(tot[p])


def _dn_inter(ci, u_scr, w_scr, qd_scr, kd_scr, qk_scr, et_scr, s_scr, o_scr):
    c = DN_CHUNK
    rows = [pl.ds(pl.multiple_of(i * c, c), c) for i in ci]
    s = [s_scr[d] for d in range(2)]
    ws = [_bdot(jnp.concatenate([w_scr[d, rows[d], :], qd_scr[d, rows[d], :]], axis=0), s[d]) for d in range(2)]
    v_new = [u_scr[d, rows[d], :] - ws[d][0:c] for d in range(2)]
    o = [ws[d][c:2 * c] + _bdot(qk_scr[d, rows[d], :], v_new[d]) for d in range(2)]
    s_new = [s[d] * et_scr[d, pl.ds(ci[d], 1), :] + _bdot_tn(kd_scr[d, rows[d], :], v_new[d]) for d in range(2)]
    for d in range(2):
        o_scr[d, rows[d], :] = o[d]
        s_scr[d] = s_new[d]


def _dn_kernel(*refs, has_state, nh):
    if has_state:
        (q_ref, k_ref, v_ref, z_ref, ab_ref, wq_ref, wk_ref, wv_ref, alog_ref, bias_ref, ng_ref, s0_ref, o_ref,
         q_scr, k_scr, v_scr, gb_scr, s_scr, o_scr, *chunk_scr) = refs
    else:
        (q_ref, k_ref, v_ref, z_ref, ab_ref, wq_ref, wk_ref, wv_ref, alog_ref, bias_ref, ng_ref, o_ref, sfin_ref,
         q_scr, k_scr, v_scr, gb_scr, s_scr, o_scr, *chunk_scr) = refs
    hd = DN_HEAD_DIM
    h = pl.program_id(1)
    l = q_ref.shape[0]
    n = l // DN_CHUNK
    q_scr[...] = _l2(_dn_conv(q_ref, wq_ref)) * hd ** -0.5
    k_scr[...] = _l2(_dn_conv(k_ref, wk_ref))
    v_scr[...] = _dn_conv(v_ref, wv_ref)
    x = ab_ref[...]
    lane = lax.broadcasted_iota(I32, x.shape, 1)
    is_decay = ((lane // nh) % 2) == 0
    mix = jnp.where(is_decay, -jnp.exp(alog_ref[...]) * _softplus(x + bias_ref[...]), _sigmoid(x))
    sr = lax.broadcasted_iota(I32, (LANES, 4 * hd), 0)
    sc = lax.broadcasted_iota(I32, (LANES, 4 * hd), 1)
    sel = (sr == LANES // 2 + nh * (sc // hd) + h).astype(F32)
    gb_scr[...] = jnp.dot(mix, sel, precision=HIGHEST, preferred_element_type=F32)
    if has_state:
        s_scr[...] = s0_ref[...]
    else:
        s_scr[...] = jnp.zeros(s_scr.shape, F32)

    per_body = min(n, 4)

    def intra(i, carry):
        _dn_intra([i * per_body + t for t in range(per_body)], q_scr, k_scr, v_scr, gb_scr, *chunk_scr)
        return carry

    if n == per_body:
        intra(0, 0)
    else:
        lax.fori_loop(0, n // per_body, intra, 0)

    def inter(c, carry):
        _dn_inter([c, n - 1 - c], *chunk_scr, s_scr, o_scr)
        return carry

    lax.fori_loop(0, n, inter, 0)
    o = _rms(o_scr[0] + o_scr[1], ng_ref[...]) * _silu(z_ref[...])
    o_ref[...] = o.astype(o_ref.dtype)
    if not has_state:
        sfin_ref[...] = s_scr[...]


def _deltanet(xin, ab_col, conv_w, a_log, dt_bias, norm_g, b, l, row_blk0, state=None, layer=0):
    hd = DN_HEAD_DIM
    nh = a_log.shape[1]
    has_state = state is not None
    lane_par = lambda p: jnp.zeros((1, LANES), F32).at[0, LANES // 2:LANES // 2 + 4 * nh].set(
        jnp.concatenate([p[0], jnp.zeros_like(p[0]), p[1], jnp.zeros_like(p[1])]))
    xspec = lambda sec: pl.BlockSpec((l, hd), lambda i, h: (row_blk0 + i, sec * nh + h))
    wspec = lambda sec: pl.BlockSpec((DN_CONV, hd), lambda i, h: (0, sec * nh + h))
    vec = pl.BlockSpec((1, LANES), lambda i, h: (0, 0))
    in_specs = [xspec(0), xspec(1), xspec(2), xspec(3),
                pl.BlockSpec((l, LANES), lambda i, h: (row_blk0 + i, ab_col // LANES)),
                wspec(0), wspec(1), wspec(2), vec, vec, vec]
    args = [xin, xin, xin, xin, xin, conv_w, conv_w, conv_w, lane_par(a_log), lane_par(dt_bias),
            norm_g.reshape(1, hd)]
    o_shape = jax.ShapeDtypeStruct((b * l, nh * hd), BF16)
    o_spec = pl.BlockSpec((l, hd), lambda i, h: (i, h))
    if has_state:
        in_specs.append(pl.BlockSpec((None, None, 2, None, hd, hd), lambda i, h: (i, layer, 0, h, 0, 0)))
        args.append(state)
        out_shape, out_specs = o_shape, o_spec
    else:
        out_shape = (o_shape, jax.ShapeDtypeStruct((b, 2, nh, hd, hd), F32))
        out_specs = (o_spec, pl.BlockSpec((None, 2, None, hd, hd), lambda i, h: (i, 0, h, 0, 0)))
    scratch = [pltpu.VMEM((l, hd), F32), pltpu.VMEM((l, hd), F32), pltpu.VMEM((l, hd), F32),
               pltpu.VMEM((l, 4 * hd), F32), pltpu.VMEM((2, hd, hd), F32), pltpu.VMEM((2, l, hd), F32)]
    scratch += [pltpu.VMEM((2, l, hd), F32)] * 4 + [pltpu.VMEM((2, l, DN_CHUNK), F32),
                                                    pltpu.VMEM((2, l // DN_CHUNK, hd), F32)]
    return pl.pallas_call(
        functools.partial(_dn_kernel, has_state=has_state, nh=nh), name="deltanet_lat" if has_state else "deltanet_ctx",
        out_shape=out_shape, grid=(b, nh), in_specs=in_specs, out_specs=out_specs, scratch_shapes=scratch,
        compiler_params=_cparams(2),
    )(*args)


def _round_up(n, k):
    return -(-n // k) * k


def kernel(x_prompt, x_sample, cache_gqa_k, cache_gqa_v, cache_mla_ckv, cache_mla_krope, state_dn, state_ssm_re, state_ssm_im, c, c_ctx, w_ada, b_ada, norm1_g, norm2_g, w_in, dn_conv, dn_a_log, dn_dt_bias, dn_norm_g, ssm_lam_re, ssm_lam_im, ssm_log_dt, ssm_b_re, ssm_b_im, ssm_c_re, ssm_c_im, ssm_d, ssm_w_glu, ssm_b_glu, gqa_q_norm, gqa_k_norm, mla_q_norm, mla_kv_norm, mla_w_uq, mla_w_ukv, w_out, moe_w_router, moe_b_router, moe_w_gate, moe_b_gate, moe_w_up, moe_b_up, moe_w_down, moe_b_down, norm_f):
    bc, lc, d = x_prompt.shape
    bl, ll, _ = x_sample.shape
    depth = w_in.shape[0]
    m_ctx = bc * lc
    assert m_ctx % ll == 0 and lc % DN_CHUNK == 0 and ll % DN_CHUNK == 0
    lat_blk0 = m_ctx // ll
    gw = d // 4
    nh = dn_a_log.shape[2]
    n_gqa = gw + 2 * GQA_KV * GQA_HD
    n_mla = MLA_Q_LORA + MLA_KV_LORA + MLA_ROPE
    dn_in = 4 * gw + 4 * nh
    assert w_in.shape[2] == dn_in + gw + n_gqa + n_mla

    c_ssm = 4 * gw
    c_gqa = c_ssm + gw
    q0, k0, v0 = c_gqa, c_gqa + gw, c_gqa + gw + GQA_KV * GQA_HD
    c_mla = _round_up(c_gqa + n_gqa, MLA_Q_LORA)
    c_kr = c_mla + MLA_Q_LORA + MLA_KV_LORA
    assert (c_mla + MLA_Q_LORA) % MLA_KV_LORA == 0 and c_kr % LANES == 0 and MLA_ROPE + 4 * nh <= LANES
    n_w = _round_up(c_kr + LANES, 512)
    o_ssm, o_gqa, o_mla = dn_in, dn_in + gw, dn_in + gw + n_gqa
    zpad = lambda n: jnp.zeros((depth, d, n), F32)
    w_in_p = jnp.concatenate(
        [w_in[:, :, :4 * gw], w_in[:, :, o_ssm:o_gqa], w_in[:, :, o_gqa:o_mla], zpad(c_mla - c_gqa - n_gqa),
         w_in[:, :, o_mla:o_mla + MLA_Q_LORA + MLA_KV_LORA],
         w_in[:, :, o_mla + MLA_Q_LORA + MLA_KV_LORA:], zpad(LANES // 2 - MLA_ROPE), w_in[:, :, 4 * gw:dn_in],
         zpad(n_w - c_kr - LANES // 2 - 4 * nh)], axis=-1).astype(BF16)
    w_out_b = w_out.astype(BF16)

    x = jnp.concatenate([x_prompt.reshape(m_ctx, d), x_sample.reshape(bl * ll, d)], axis=0)
    n_cond = _round_up(1 + bl, 8)
    cond = jnp.zeros((n_cond, d), F32).at[0].set(c_ctx).at[1:1 + bl].set(c)

    p_len = cache_gqa_k.shape[2]
    ck = cache_gqa_k.reshape(bl, depth, p_len, GQA_KV * GQA_HD)
    cv = cache_gqa_v.reshape(bl, depth, p_len, GQA_KV * GQA_HD)
    ckr2 = jnp.concatenate([cache_mla_krope, cache_mla_krope], axis=-1)
    n_grp, n_p = ssm_lam_re.shape[2:]

    outs = [[] for _ in range(7)]
    for l in range(depth):
        mod = _ada(cond, w_ada, b_ada[l], l).reshape(n_cond, 6, 1, d)
        xin = _inproj(x, mod, norm1_g[l], w_in_p, l, m_ctx, ll)

        dn_args = (xin, c_kr, dn_conv[l], dn_a_log[l], dn_dt_bias[l], dn_norm_g[l])
        o_dn_c, dn_fin = _deltanet(*dn_args, bc, lc, 0)
        o_dn_l = _deltanet(*dn_args, bl, ll, lat_blk0, state=state_dn, layer=l)

        ops = _s5_operators(ssm_lam_re[l], ssm_lam_im[l], ssm_log_dt[l], ssm_b_re[l], ssm_b_im[l],
                            ssm_c_re[l], ssm_c_im[l], ssm_d[l])
        u = xin[:, c_ssm:c_ssm + gw]
        y_c, ssm_fin = _s5_path(u[:m_ctx], ops, bc, lc, jnp.zeros((n_grp, 2, bc, 2 * n_p), F32))
        h0 = jnp.concatenate([state_ssm_re[:, l], state_ssm_im[:, l]], axis=-1).transpose(2, 1, 0, 3)
        y_l, _ = _s5_path(u[m_ctx:], ops, bl, ll, h0)
        o_s5 = _glu(jnp.concatenate([y_c, y_l], axis=0), ssm_w_glu[l], ssm_b_glu[l])

        gqa_args = (xin, (q0, k0, v0), gqa_q_norm[l], gqa_k_norm[l])
        o_gqa_c, k_new = _gqa(*gqa_args, bc, lc, 0)
        o_gqa_l = _gqa(*gqa_args, bl, ll, lat_blk0, cache=(ck, cv), layer=l)

        mla_args = (xin, c_mla, mla_q_norm[l], mla_kv_norm[l], mla_w_uq[l], mla_w_ukv[l])
        o_mla_c, ckv_new = _mla(*mla_args, bc, lc, 0)
        o_mla_l = _mla(*mla_args, bl, ll, lat_blk0, cache=(cache_mla_ckv, ckr2), layer=l)

        cat = lambda a, b_: jnp.concatenate([a, b_], axis=0)
        parts = [cat(o_dn_c, o_dn_l), o_s5, cat(o_gqa_c, o_gqa_l), cat(o_mla_c, o_mla_l)]
        x = _outproj(parts, w_out_b, l, x, mod, m_ctx, ll)
        x = _moe(x, mod, norm2_g[l], moe_w_router[l], moe_b_router[l], moe_w_gate, moe_b_gate,
                 moe_w_up, moe_b_up, moe_w_down, moe_b_down, l, norm_f, m_ctx, ll,
                 final_norm=(l == depth - 1))

        fin = ssm_fin.transpose(2, 1, 0, 3)
        outs[0].append(k_new.reshape(bc, lc, GQA_KV, GQA_HD))
        outs[1].append(xin[:m_ctx, v0:v0 + GQA_KV * GQA_HD].reshape(bc, lc, GQA_KV, GQA_HD))
        outs[2].append(ckv_new.reshape(bc, lc, MLA_KV_LORA))
        outs[3].append(xin[:m_ctx, c_kr:c_kr + MLA_ROPE].reshape(bc, lc, MLA_ROPE))
        outs[4].append(dn_fin)
        outs[5].append(fin[..., :n_p])
        outs[6].append(fin[..., n_p:])

    y_prompt = x[:m_ctx].reshape(bc, lc, d)
    y_sample = x[m_ctx:].reshape(bl, ll, d)
    return (y_prompt, y_sample) + tuple(jnp.stack(o, axis=1) for o in outs)
```
